```python
import jax, jax.numpy as jnp
from jax import lax
import numpy as np

D_MODEL = 2048
BATCH = 2
SEQ = 8192
DEPTH = 1

CHUNK = 64
D_MIX = D_MODEL
CONV_WIDTH = D_MIX // 2
CONV_HEADS = 8
CONV_HEAD_DIM = CONV_WIDTH // CONV_HEADS
CONV_K = 3
POOL_WIDTH = D_MIX - CONV_WIDTH
POOL_WINDOWS = (2, 4, 8, 16)
N_POOL_GROUPS = len(POOL_WINDOWS)
POOL_GROUP_DIM = POOL_WIDTH // N_POOL_GROUPS
IN_PROJ_WIDTH = 3 * CONV_WIDTH + POOL_WIDTH
D_FF = ((8 * D_MODEL // 3 + 255) // 256) * 256
EPS = 1e-6

kernel_name = "hybrid_shortconv_multiscale_pool_block"


def rms_norm(x, g):
    xf = x.astype(jnp.float32)
    y = xf * lax.rsqrt(jnp.mean(xf * xf, axis=-1, keepdims=True) + EPS)
    return (y * g.astype(jnp.float32)).astype(x.dtype)


def rms_norm_plain(x):
    xf = x.astype(jnp.float32)
    y = xf * lax.rsqrt(jnp.mean(xf * xf, axis=-1, keepdims=True) + EPS)
    return y.astype(x.dtype)


def short_conv_causal(u, w):
    c = u.shape[-1]
    rhs = w[:, None, :].astype(u.dtype)
    return lax.conv_general_dilated(
        u, rhs, window_strides=(1,), padding=[(CONV_K - 1, 0)],
        dimension_numbers=("NWC", "WIO", "NWC"), feature_group_count=c)


def multiscale_pool_causal(v):
    bn, s, _ = v.shape
    vg = v.reshape(bn, s, N_POOL_GROUPS, POOL_GROUP_DIM).astype(jnp.float32)
    cs = jnp.cumsum(vg, axis=1)
    pos = jnp.arange(1, s + 1, dtype=jnp.float32)
    outs = []
    for gi, w in enumerate(POOL_WINDOWS):
        c = cs[:, :, gi]
        prev = jnp.pad(c, ((0, 0), (w, 0), (0, 0)))[:, :s]
        cnt = jnp.minimum(pos, float(w))[None, :, None]
        outs.append((c - prev) / cnt - vg[:, :, gi])
    return jnp.stack(outs, axis=2)


def setup_inputs(seed: int = 0) -> dict:
    key = jax.random.key(seed)
    ks = jax.random.split(key, 16)
    L = DEPTH

    def nrm(k, shape, fan_in):
        return jax.random.normal(k, shape, jnp.float32) * (fan_in ** -0.5)

    def gain(k, shape):
        return 1.0 + 0.05 * jax.random.normal(k, shape, jnp.float32)

    return {
        "x": jax.random.normal(ks[0], (BATCH, SEQ, D_MODEL), jnp.float32),
        "ln_mix_pre": gain(ks[1], (L, D_MODEL)),
        "w_in": nrm(ks[2], (L, D_MODEL, IN_PROJ_WIDTH), D_MODEL),
        "conv_w": nrm(ks[3], (L, CONV_K, CONV_WIDTH), CONV_K),
        "pool_w": nrm(ks[4], (L, N_POOL_GROUPS, POOL_GROUP_DIM, POOL_GROUP_DIM), POOL_GROUP_DIM),
        "pool_scale": gain(ks[5], (L, POOL_WIDTH)),
        "w_out": nrm(ks[6], (L, D_MIX, D_MODEL), D_MIX),
        "ln_mix_post": gain(ks[7], (L, D_MODEL)),
        "ln_ffn_pre": gain(ks[8], (L, D_MODEL)),
        "w_gate": nrm(ks[9], (L, D_MODEL, D_FF), D_MODEL),
        "w_up": nrm(ks[10], (L, D_MODEL, D_FF), D_MODEL),
        "w_down": nrm(ks[11], (L, D_FF, D_MODEL), D_FF),
        "ln_ffn_post": gain(ks[12], (L, D_MODEL)),
    }


def reference(x, ln_mix_pre, w_in, conv_w, pool_w, pool_scale, w_out, ln_mix_post,
              ln_ffn_pre, w_gate, w_up, w_down, ln_ffn_post):
    bn, s, _ = x.shape
    for l in range(DEPTH):
        h = rms_norm(x, ln_mix_pre[l])
        proj = jnp.einsum("bsd,de->bse", h, w_in[l])
        gate_b, gate_c, u, v = jnp.split(
            proj, [CONV_WIDTH, 2 * CONV_WIDTH, 3 * CONV_WIDTH], axis=-1)

        y_conv = gate_b * short_conv_causal(gate_c * u, conv_w[l])
        y_conv = rms_norm_plain(y_conv.reshape(bn, s, CONV_HEADS, CONV_HEAD_DIM))
        y_conv = y_conv.reshape(bn, s, CONV_WIDTH)

        pooled = multiscale_pool_causal(v).astype(v.dtype)
        y_pool = jnp.einsum("bsgc,gcd->bsgd", pooled, pool_w[l])
        y_pool = rms_norm_plain(y_pool).reshape(bn, s, POOL_WIDTH) * pool_scale[l]

        mixed = jnp.concatenate([y_conv, y_pool], axis=-1)
        mix_out = jnp.einsum("bse,ed->bsd", mixed, w_out[l])
        x = x + rms_norm(mix_out, ln_mix_post[l])

        hf = rms_norm(x, ln_ffn_pre[l])
        g = jnp.einsum("bsd,df->bsf", hf, w_gate[l])
        up = jnp.einsum("bsd,df->bsf", hf, w_up[l])
        ff = jnp.einsum("bsf,fd->bsd", jax.nn.silu(g) * up, w_down[l])
        x = x + rms_norm(ff, ln_ffn_post[l])
    return x
```

```python
import functools

import jax
import jax.numpy as jnp
from jax import lax
from jax.experimental import pallas as pl
from jax.experimental.pallas import tpu as pltpu

EPS = 1e-6
CONV_HEADS = 8
CONV_K = 3
POOL_WINDOWS = (2, 4, 8, 16)
HALO = 16

V7X_VMEM_LIMIT_BYTES = 56 * 1024 * 1024


def _rms(xf, axis_size):
    return lax.rsqrt(jnp.sum(xf * xf, axis=-1, keepdims=True) * (1.0 / axis_size) + EPS)


def _mixer_kernel(x_ref, g_pre_ref, w_in_ref, conv_w_ref, pool_w_ref, pool_scale_ref,
                  w_out_ref, g_post_ref, o_ref, cu_buf, v_buf, mixed_buf, *, ts, d_model, cw, pw):
    s = pl.program_id(1)
    n_groups = len(POOL_WINDOWS)
    gd = pw // n_groups
    hd = cw // CONV_HEADS

    @pl.when(s == 0)
    def _():
        cu_buf[0:HALO, :] = jnp.zeros((HALO, cw), jnp.float32)
        v_buf[0:HALO, :] = jnp.zeros((HALO, pw), jnp.float32)

    @pl.when(s != 0)
    def _():
        cu_buf[0:HALO, :] = cu_buf[ts:ts + HALO, :]
        v_buf[0:HALO, :] = v_buf[ts:ts + HALO, :]

    x = x_ref[0]
    h = (x * _rms(x, d_model) * g_pre_ref[...]).astype(jnp.bfloat16)

    def proj(lo, width):
        return jnp.dot(h, w_in_ref[:, lo:lo + width], preferred_element_type=jnp.float32)

    cu_buf[HALO:HALO + ts, :] = proj(cw, cw) * proj(2 * cw, cw)
    conv = conv_w_ref[CONV_K - 1:CONV_K, :] * cu_buf[HALO:HALO + ts, :]
    for k in range(CONV_K - 1):
        shift = CONV_K - 1 - k
        conv = conv + conv_w_ref[k:k + 1, :] * cu_buf[HALO - shift:HALO - shift + ts, :]
    y_conv = proj(0, cw) * conv
    for hh in range(CONV_HEADS):
        yh = y_conv[:, hh * hd:(hh + 1) * hd]
        mixed_buf[:, hh * hd:(hh + 1) * hd] = (yh * _rms(yh, hd)).astype(jnp.bfloat16)

    v_buf[HALO:HALO + ts, :] = proj(3 * cw, pw)
    pos = s * ts + lax.broadcasted_iota(jnp.int32, (ts, 1), 0) + 1
    for gi, w in enumerate(POOL_WINDOWS):
        lo = gi * gd
        vg = v_buf[HALO:HALO + ts, lo:lo + gd]
        if w <= 8:
            win = vg
            for j in range(1, w):
                win = win + v_buf[HALO - j:HALO - j + ts, lo:lo + gd]
        else:
            p8 = v_buf[HALO - 8:HALO + ts, lo:lo + gd]
            for j in range(1, 8):
                p8 = p8 + v_buf[HALO - 8 - j:HALO - j + ts, lo:lo + gd]
            win = p8[8:, :] + p8[:ts, :]
        cnt = jnp.minimum(pos, w).astype(jnp.float32)
        pooled = (win / cnt - vg).astype(jnp.bfloat16)
        yp = jnp.dot(pooled, pool_w_ref[gi], preferred_element_type=jnp.float32)
        yp = yp * _rms(yp, gd) * pool_scale_ref[:, lo:lo + gd]
        mixed_buf[:, cw + lo:cw + lo + gd] = yp.astype(jnp.bfloat16)

    mix_out = jnp.dot(mixed_buf[...], w_out_ref[...], preferred_element_type=jnp.float32)
    o_ref[0] = x + mix_out * _rms(mix_out, d_model) * g_post_ref[...]


def _mixer(x, g_pre, w_in, conv_w, pool_w, pool_scale, w_out, g_post, *, ts):
    bn, seq, d_model = x.shape
    cw = conv_w.shape[-1]
    pw = pool_scale.shape[-1]
    n_groups, gd, _ = pool_w.shape
    resident = pl.Buffered(1)

    def const(shape):
        return pl.BlockSpec(shape, lambda b, s: (0,) * len(shape), pipeline_mode=resident)

    kern = functools.partial(_mixer_kernel, ts=ts, d_model=d_model, cw=cw, pw=pw)
    return pl.pallas_call(
        kern,
        grid=(bn, seq // ts),
        in_specs=[
            pl.BlockSpec((1, ts, d_model), lambda b, s: (b, s, 0)),
            const((1, d_model)),
            const(w_in.shape),
            const(conv_w.shape),
            const(pool_w.shape),
            const((1, pw)),
            const(w_out.shape),
            const((1, d_model)),
        ],
        out_specs=pl.BlockSpec((1, ts, d_model), lambda b, s: (b, s, 0)),
        out_shape=jax.ShapeDtypeStruct(x.shape, x.dtype),
        scratch_shapes=[
            pltpu.VMEM((ts + HALO, cw), jnp.float32),
            pltpu.VMEM((ts + HALO, pw), jnp.float32),
            pltpu.VMEM((ts, cw + pw), jnp.bfloat16),
        ],
        compiler_params=pltpu.CompilerParams(
            dimension_semantics=("arbitrary", "arbitrary"),
            vmem_limit_bytes=V7X_VMEM_LIMIT_BYTES),
        name="mixer",
    )(x, g_pre.reshape(1, -1), w_in, conv_w, pool_w, pool_scale.reshape(1, -1), w_out,
      g_post.reshape(1, -1))


def _ffn_kernel(x_ref, g_pre_ref, wg_ref, wu_ref, wd_ref, g_post_ref, o_ref, hf_buf, acc_buf,
                *, d_model):
    j = pl.program_id(1)

    @pl.when(j == 0)
    def _():
        x = x_ref[...]
        hf_buf[...] = (x * _rms(x, d_model) * g_pre_ref[...]).astype(jnp.bfloat16)
        acc_buf[...] = jnp.zeros_like(acc_buf)

    hf = hf_buf[...]
    g = jnp.dot(hf, wg_ref[...], preferred_element_type=jnp.float32)
    up = jnp.dot(hf, wu_ref[...], preferred_element_type=jnp.float32)
    act = (g * jax.nn.sigmoid(g) * up).astype(jnp.bfloat16)
    acc_buf[...] += jnp.dot(act, wd_ref[...], preferred_element_type=jnp.float32)

    @pl.when(j == pl.num_programs(1) - 1)
    def _():
        ff = acc_buf[...]
        o_ref[...] = x_ref[...] + ff * _rms(ff, d_model) * g_post_ref[...]


def _ffn(x, g_pre, w_gate, w_up, w_down, g_post, *, tm, tf):
    m, d_model = x.shape
    d_ff = w_gate.shape[-1]
    kern = functools.partial(_ffn_kernel, d_model=d_model)
    return pl.pallas_call(
        kern,
        grid=(m // tm, d_ff // tf),
        in_specs=[
            pl.BlockSpec((tm, d_model), lambda i, j: (i, 0)),
            pl.BlockSpec((1, d_model), lambda i, j: (0, 0)),
            pl.BlockSpec((d_model, tf), lambda i, j: (0, j)),
            pl.BlockSpec((d_model, tf), lambda i, j: (0, j)),
            pl.BlockSpec((tf, d_model), lambda i, j: (j, 0)),
            pl.BlockSpec((1, d_model), lambda i, j: (0, 0)),
        ],
        out_specs=pl.BlockSpec((tm, d_model), lambda i, j: (i, 0)),
        out_shape=jax.ShapeDtypeStruct(x.shape, x.dtype),
        scratch_shapes=[
            pltpu.VMEM((tm, d_model), jnp.bfloat16),
            pltpu.VMEM((tm, d_model), jnp.float32),
        ],
        compiler_params=pltpu.CompilerParams(
            dimension_semantics=("arbitrary", "arbitrary"),
            vmem_limit_bytes=V7X_VMEM_LIMIT_BYTES),
        name="ffn",
    )(x, g_pre.reshape(1, -1), w_gate, w_up, w_down, g_post.reshape(1, -1))


def kernel(x, ln_mix_pre, w_in, conv_w, pool_w, pool_scale, w_out, ln_mix_post, ln_ffn_pre,
           w_gate, w_up, w_down, ln_ffn_post):
    bn, seq, d_model = x.shape
    depth = w_in.shape[0]
    bf = jnp.bfloat16
    for l in range(depth):
        x = _mixer(x, ln_mix_pre[l], w_in[l].astype(bf), conv_w[l], pool_w[l].astype(bf),
                   pool_scale[l], w_out[l].astype(bf), ln_mix_post[l], ts=256)
        y = _ffn(x.reshape(bn * seq, d_model), ln_ffn_pre[l], w_gate[l].astype(bf),
                 w_up[l].astype(bf), w_down[l].astype(bf), ln_ffn_post[l], tm=512, tf=512)
        x = y.reshape(bn, seq, d_model)
    return x
```

```python
import functools

import jax
import jax.numpy as jnp
from jax import lax
from jax.experimental import pallas as pl
from jax.experimental.pallas import tpu as pltpu

EPS = 1e-6
CONV_HEADS = 8
CONV_K = 3
POOL_WINDOWS = (2, 4, 8, 16)
HALO = 16

V7X_VMEM_LIMIT_BYTES = 56 * 1024 * 1024


def _rms(xf, axis_size):
    return lax.rsqrt(jnp.sum(xf * xf, axis=-1, keepdims=True) * (1.0 / axis_size) + EPS)


def _mixer_kernel(x_ref, xlag_ref, g_pre_ref, w_in_ref, conv_w_ref, pool_w_ref, pool_scale_ref,
                  w_out_ref, g_post_ref, wg_ref, wu_ref, wd_ref,
                  o_ref, wg_o, wu_o, wd_o,
                  cu_hist, v_hist, mixed_buf, *, ts, tiles_per_seq, n_tiles, d_model, cw, pw):
    t = pl.program_id(0)
    n_groups = len(POOL_WINDOWS)
    gd = pw // n_groups
    hd = cw // CONV_HEADS

    def back():
        mix_out = jnp.dot(mixed_buf[...], w_out_ref[...], preferred_element_type=jnp.float32)
        o_ref[0] = xlag_ref[0] + mix_out * _rms(mix_out, d_model) * g_post_ref[...]

    def front(first_step):
        s = lax.rem(t, tiles_per_seq)
        x = x_ref[0]
        h = (x * _rms(x, d_model) * g_pre_ref[...]).astype(jnp.bfloat16)

        def proj(lo, width):
            return jnp.dot(h, w_in_ref[:, lo:lo + width], preferred_element_type=jnp.float32)

        def history(hist_ref):
            if first_step:
                return jnp.zeros(hist_ref.shape, jnp.float32)
            return jnp.where(s == 0, 0.0, hist_ref[...])

        v = proj(3 * cw, pw)
        v_ext = jnp.concatenate([history(v_hist), v], axis=0)
        v_hist[...] = v[ts - HALO:, :]

        cu = proj(cw, cw) * proj(2 * cw, cw)
        cu_ext = jnp.concatenate([history(cu_hist), cu], axis=0)
        cu_hist[...] = cu[ts - HALO:, :]
        conv = conv_w_ref[CONV_K - 1:CONV_K, :] * cu
        for k in range(CONV_K - 1):
            delay = CONV_K - 1 - k
            conv = conv + conv_w_ref[k:k + 1, :] * pltpu.roll(cu_ext, delay, axis=0)[HALO:, :]
        y_conv = proj(0, cw) * conv
        for hh in range(CONV_HEADS):
            yh = y_conv[:, hh * hd:(hh + 1) * hd]
            mixed_buf[:, hh * hd:(hh + 1) * hd] = (yh * _rms(yh, hd)).astype(jnp.bfloat16)

        pos = s * ts + lax.broadcasted_iota(jnp.int32, (ts, 1), 0) + 1
        for gi, w in enumerate(POOL_WINDOWS):
            lo = gi * gd
            win = v_ext[:, lo:lo + gd]
            k = 1
            while k < w:
                win = win + (pltpu.roll(win, k, axis=0) if k % 8 else
                             jnp.concatenate([win[:k], win[:-k]], axis=0))
                k *= 2
            cnt = jnp.minimum(pos, w).astype(jnp.float32)
            pooled = (win[HALO:, :] / cnt - v[:, lo:lo + gd]).astype(jnp.bfloat16)
            yp = jnp.dot(pooled, pool_w_ref[gi], preferred_element_type=jnp.float32)
            yp = yp * _rms(yp, gd) * pool_scale_ref[:, lo:lo + gd]
            mixed_buf[:, cw + lo:cw + lo + gd] = yp.astype(jnp.bfloat16)

        wg_o[...] = wg_ref[...].astype(jnp.bfloat16)
        wu_o[...] = wu_ref[...].astype(jnp.bfloat16)
        wd_o[...] = wd_ref[...].astype(jnp.bfloat16)

    @pl.when(t == 0)
    def _():
        front(True)

    @pl.when(jnp.logical_and(t > 0, t < n_tiles))
    def _():
        back()
        front(False)

    @pl.when(t == n_tiles)
    def _():
        back()


def _mixer(x, g_pre, w_in, conv_w, pool_w, pool_scale, w_out, g_post, w_gate, w_up, w_down, *, ts):
    bn, seq, d_model = x.shape
    cw = conv_w.shape[-1]
    pw = pool_scale.shape[-1]
    d_ff = w_gate.shape[-1]
    tiles_per_seq = seq // ts
    n_tiles = bn * tiles_per_seq
    gu_rows = d_model // n_tiles
    wd_rows = 2 * d_ff // n_tiles
    assert gu_rows * n_tiles == d_model and gu_rows % 16 == 0
    assert wd_rows * n_tiles == 2 * d_ff and wd_rows % 16 == 0

    def const(shape):
        return pl.BlockSpec(shape, lambda t: (0,) * len(shape), pipeline_mode=pl.Buffered(1))

    def tile_map(lag):
        def index_map(t):
            i = jnp.clip(t - lag, 0, n_tiles - 1)
            return (i // tiles_per_seq, i % tiles_per_seq, 0)
        return index_map

    def gu_map(t):
        return (jnp.minimum(t, n_tiles - 1), 0)

    def wd_map(t):
        return (jnp.minimum(t, n_tiles - 1) // 2, 0)

    kern = functools.partial(_mixer_kernel, ts=ts, tiles_per_seq=tiles_per_seq, n_tiles=n_tiles,
                             d_model=d_model, cw=cw, pw=pw)
    bf = jnp.bfloat16
    return pl.pallas_call(
        kern,
        grid=(n_tiles + 1,),
        in_specs=[
            pl.BlockSpec((1, ts, d_model), tile_map(0)),
            pl.BlockSpec((1, ts, d_model), tile_map(1)),
            const((1, d_model)),
            const(w_in.shape),
            const(conv_w.shape),
            const(pool_w.shape),
            const((1, pw)),
            const(w_out.shape),
            const((1, d_model)),
            pl.BlockSpec((gu_rows, d_ff), gu_map),
            pl.BlockSpec((gu_rows, d_ff), gu_map),
            pl.BlockSpec((wd_rows, d_model), wd_map),
        ],
        out_specs=[
            pl.BlockSpec((1, ts, d_model), tile_map(1)),
            pl.BlockSpec((gu_rows, d_ff), gu_map),
            pl.BlockSpec((gu_rows, d_ff), gu_map),
            pl.BlockSpec((wd_rows, d_model), wd_map),
        ],
        out_shape=[
            jax.ShapeDtypeStruct(x.shape, x.dtype),
            jax.ShapeDtypeStruct(w_gate.shape, bf),
            jax.ShapeDtypeStruct(w_up.shape, bf),
            jax.ShapeDtypeStruct(w_down.shape, bf),
        ],
        scratch_shapes=[
            pltpu.VMEM((HALO, cw), jnp.float32),
            pltpu.VMEM((HALO, pw), jnp.float32),
            pltpu.VMEM((ts, cw + pw), jnp.bfloat16),
        ],
        compiler_params=pltpu.CompilerParams(
            dimension_semantics=("arbitrary",),
            vmem_limit_bytes=V7X_VMEM_LIMIT_BYTES),
        name="mixer",
    )(x, x, g_pre.reshape(1, -1), w_in, conv_w, pool_w, pool_scale.reshape(1, -1), w_out,
      g_post.reshape(1, -1), w_gate, w_up, w_down)


def _ffn_kernel(x_ref, g_pre_ref, wg_ref, wu_ref, wd_ref, g_post_ref, o_ref, hf_buf, acc_buf,
                *, d_model):
    j = pl.program_id(1)

    @pl.when(j == 0)
    def _():
        x = x_ref[...]
        hf_buf[...] = (x * _rms(x, d_model) * g_pre_ref[...]).astype(jnp.bfloat16)
        acc_buf[...] = jnp.zeros_like(acc_buf)

    hf = hf_buf[...]
    g = jnp.dot(hf, wg_ref[...], preferred_element_type=jnp.float32)
    up = jnp.dot(hf, wu_ref[...], preferred_element_type=jnp.float32)
    act = (g * jax.nn.sigmoid(g) * up).astype(jnp.bfloat16)
    acc_buf[...] += jnp.dot(act, wd_ref[...], preferred_element_type=jnp.float32)

    @pl.when(j == pl.num_programs(1) - 1)
    def _():
        ff = acc_buf[...]
        o_ref[...] = x_ref[...] + ff * _rms(ff, d_model) * g_post_ref[...]


def _ffn(x, g_pre, w_gate, w_up, w_down, g_post, *, tm, tf):
    m, d_model = x.shape
    d_ff = w_gate.shape[-1]
    kern = functools.partial(_ffn_kernel, d_model=d_model)
    return pl.pallas_call(
        kern,
        grid=(m // tm, d_ff // tf),
        in_specs=[
            pl.BlockSpec((tm, d_model), lambda i, j: (i, 0)),
            pl.BlockSpec((1, d_model), lambda i, j: (0, 0)),
            pl.BlockSpec((d_model, tf), lambda i, j: (0, j)),
            pl.BlockSpec((d_model, tf), lambda i, j: (0, j)),
            pl.BlockSpec((tf, d_model), lambda i, j: (j, 0)),
            pl.BlockSpec((1, d_model), lambda i, j: (0, 0)),
        ],
        out_specs=pl.BlockSpec((tm, d_model), lambda i, j: (i, 0)),
        out_shape=jax.ShapeDtypeStruct(x.shape, x.dtype),
        scratch_shapes=[
            pltpu.VMEM((tm, d_model), jnp.bfloat16),
            pltpu.VMEM((tm, d_model), jnp.float32),
        ],
        compiler_params=pltpu.CompilerParams(
            dimension_semantics=("arbitrary", "arbitrary"),
            vmem_limit_bytes=V7X_VMEM_LIMIT_BYTES),
        name="ffn",
    )(x, g_pre.reshape(1, -1), w_gate, w_up, w_down, g_post.reshape(1, -1))


def kernel(x, ln_mix_pre, w_in, conv_w, pool_w, pool_scale, w_out, ln_mix_post, ln_ffn_pre,
           w_gate, w_up, w_down, ln_ffn_post):
    bn, seq, d_model = x.shape
    depth = w_in.shape[0]
    bf = jnp.bfloat16
    for l in range(depth):
        x, wg, wu, wd = _mixer(x, ln_mix_pre[l], w_in[l].astype(bf), conv_w[l], pool_w[l].astype(bf),
                               pool_scale[l], w_out[l].astype(bf), ln_mix_post[l],
                               w_gate[l], w_up[l], w_down[l], ts=256)
        y = _ffn(x.reshape(bn * seq, d_model), ln_ffn_pre[l], wg, wu, wd, ln_ffn_post[l],
                 tm=512, tf=512)
        x = y.reshape(bn, seq, d_model)
    return x
```

```python
import functools

import jax
import jax.numpy as jnp
from jax import lax
from jax.experimental import pallas as pl
from jax.experimental.pallas import tpu as pltpu

EPS = 1e-6
CONV_HEADS = 8
CONV_K = 3
POOL_WINDOWS = (2, 4, 8, 16)
HALO = 16

V7X_VMEM_LIMIT_BYTES = 56 * 1024 * 1024


def _rms(xf, axis_size):
    return lax.rsqrt(jnp.sum(xf * xf, axis=-1, keepdims=True) * (1.0 / axis_size) + EPS)


def _mixer_kernel(x_ref, xlag_ref, g_pre_ref, w_in_ref, conv_w_ref, pool_w_ref, pool_scale_ref,
                  w_out_ref, g_post_ref, wg_ref, wu_ref, wd_ref,
                  o_ref, wg_o, wu_o, wd_o,
                  cu_hist, v_hist, mixed_buf, *, ts, tiles_per_seq, n_tiles, d_model, cw, pw):
    t = pl.program_id(0)
    n_groups = len(POOL_WINDOWS)
    gd = pw // n_groups
    hd = cw // CONV_HEADS

    def back():
        mix_out = jnp.dot(mixed_buf[...], w_out_ref[...], preferred_element_type=jnp.float32)
        o_ref[0] = xlag_ref[0] + mix_out * _rms(mix_out, d_model) * g_post_ref[...]

    def front(first_step):
        s = lax.rem(t, tiles_per_seq)
        x = x_ref[0]
        h = (x * _rms(x, d_model) * g_pre_ref[...]).astype(jnp.bfloat16)

        def proj(lo, width):
            return jnp.dot(h, w_in_ref[:, lo:lo + width], preferred_element_type=jnp.float32)

        def history(hist_ref):
            if first_step:
                return jnp.zeros(hist_ref.shape, jnp.float32)
            return jnp.where(s == 0, 0.0, hist_ref[...])

        v = proj(3 * cw, pw)
        v_ext = jnp.concatenate([history(v_hist), v], axis=0)
        v_hist[...] = v[ts - HALO:, :]

        cu = proj(cw, cw) * proj(2 * cw, cw)
        cu_ext = jnp.concatenate([history(cu_hist), cu], axis=0)
        cu_hist[...] = cu[ts - HALO:, :]
        conv = conv_w_ref[CONV_K - 1:CONV_K, :] * cu
        for k in range(CONV_K - 1):
            delay = CONV_K - 1 - k
            conv = conv + conv_w_ref[k:k + 1, :] * pltpu.roll(cu_ext, delay, axis=0)[HALO:, :]
        y_conv = proj(0, cw) * conv
        for hh in range(CONV_HEADS):
            yh = y_conv[:, hh * hd:(hh + 1) * hd]
            mixed_buf[:, hh * hd:(hh + 1) * hd] = (yh * _rms(yh, hd)).astype(jnp.bfloat16)

        pos = s * ts + lax.broadcasted_iota(jnp.int32, (ts, 1), 0) + 1
        for gi, w in enumerate(POOL_WINDOWS):
            lo = gi * gd
            win = v_ext[:, lo:lo + gd]
            k = 1
            while k < w:
                win = win + (pltpu.roll(win, k, axis=0) if k % 8 else
                             jnp.concatenate([win[:k], win[:-k]], axis=0))
                k *= 2
            cnt = jnp.minimum(pos, w).astype(jnp.float32)
            pooled = (win[HALO:, :] / cnt - v[:, lo:lo + gd]).astype(jnp.bfloat16)
            yp = jnp.dot(pooled, pool_w_ref[gi], preferred_element_type=jnp.float32)
            yp = yp * _rms(yp, gd) * pool_scale_ref[:, lo:lo + gd]
            mixed_buf[:, cw + lo:cw + lo + gd] = yp.astype(jnp.bfloat16)

        wg_o[...] = wg_ref[...].astype(jnp.bfloat16)
        wu_o[...] = wu_ref[...].astype(jnp.bfloat16)
        wd_o[...] = wd_ref[...].astype(jnp.bfloat16)

    @pl.when(t == 0)
    def _():
        front(True)

    @pl.when(jnp.logical_and(t > 0, t < n_tiles))
    def _():
        back()
        front(False)

    @pl.when(t == n_tiles)
    def _():
        back()


def _mixer(x, g_pre, w_in, conv_w, pool_w, pool_scale, w_out, g_post, w_gate, w_up, w_down, *, ts):
    bn, seq, d_model = x.shape
    cw = conv_w.shape[-1]
    pw = pool_scale.shape[-1]
    d_ff = w_gate.shape[-1]
    tiles_per_seq = seq // ts
    n_tiles = bn * tiles_per_seq
    gu_rows = d_model // n_tiles
    wd_rows = 2 * d_ff // n_tiles
    assert gu_rows * n_tiles == d_model and gu_rows % 16 == 0
    assert wd_rows * n_tiles == 2 * d_ff and wd_rows % 16 == 0

    def const(shape):
        return pl.BlockSpec(shape, lambda t: (0,) * len(shape), pipeline_mode=pl.Buffered(1))

    def tile_map(lag):
        def index_map(t):
            i = jnp.clip(t - lag, 0, n_tiles - 1)
            return (i // tiles_per_seq, i % tiles_per_seq, 0)
        return index_map

    def gu_map(t):
        return (jnp.minimum(t, n_tiles - 1), 0)

    def wd_map(t):
        return (jnp.minimum(t, n_tiles - 1) // 2, 0)

    kern = functools.partial(_mixer_kernel, ts=ts, tiles_per_seq=tiles_per_seq, n_tiles=n_tiles,
                             d_model=d_model, cw=cw, pw=pw)
    bf = jnp.bfloat16
    return pl.pallas_call(
        kern,
        grid=(n_tiles + 1,),
        in_specs=[
            pl.BlockSpec((1, ts, d_model), tile_map(0)),
            pl.BlockSpec((1, ts, d_model), tile_map(1)),
            const((1, d_model)),
            const(w_in.shape),
            const(conv_w.shape),
            const(pool_w.shape),
            const((1, pw)),
            const(w_out.shape),
            const((1, d_model)),
            pl.BlockSpec((gu_rows, d_ff), gu_map),
            pl.BlockSpec((gu_rows, d_ff), gu_map),
            pl.BlockSpec((wd_rows, d_model), wd_map),
        ],
        out_specs=[
            pl.BlockSpec((1, ts, d_model), tile_map(1)),
            pl.BlockSpec((gu_rows, d_ff), gu_map),
            pl.BlockSpec((gu_rows, d_ff), gu_map),
            pl.BlockSpec((wd_rows, d_model), wd_map),
        ],
        out_shape=[
            jax.ShapeDtypeStruct(x.shape, x.dtype),
            jax.ShapeDtypeStruct(w_gate.shape, bf),
            jax.ShapeDtypeStruct(w_up.shape, bf),
            jax.ShapeDtypeStruct(w_down.shape, bf),
        ],
        scratch_shapes=[
            pltpu.VMEM((HALO, cw), jnp.float32),
            pltpu.VMEM((HALO, pw), jnp.float32),
            pltpu.VMEM((ts, cw + pw), jnp.bfloat16),
        ],
        compiler_params=pltpu.CompilerParams(
            dimension_semantics=("arbitrary",),
            vmem_limit_bytes=V7X_VMEM_LIMIT_BYTES),
        name="mixer",
    )(x, x, g_pre.reshape(1, -1), w_in, conv_w, pool_w, pool_scale.reshape(1, -1), w_out,
      g_post.reshape(1, -1), w_gate, w_up, w_down)


def _ffn_kernel(x_ref, g_pre_ref, wg_ref, wu_ref, wd_ref, g_post_ref, o_ref, hf_buf,
                *, d_model, tf, tail):
    j = pl.program_id(1)
    last = pl.num_programs(1) - 1

    def partial_ffn(hf, width):
        g = jnp.dot(hf, wg_ref[:, :width], preferred_element_type=jnp.float32)
        up = jnp.dot(hf, wu_ref[:, :width], preferred_element_type=jnp.float32)
        act = (g * jax.nn.sigmoid(g) * up).astype(jnp.bfloat16)
        return jnp.dot(act, wd_ref[:width, :], preferred_element_type=jnp.float32)

    @pl.when(j == 0)
    def _():
        x = x_ref[...]
        hf = (x * _rms(x, d_model) * g_pre_ref[...]).astype(jnp.bfloat16)
        hf_buf[...] = hf
        o_ref[...] = partial_ffn(hf, tf)

    @pl.when(jnp.logical_and(j > 0, j < last))
    def _():
        o_ref[...] += partial_ffn(hf_buf[...], tf)

    @pl.when(j == last)
    def _():
        ff = o_ref[...] + partial_ffn(hf_buf[...], tail)
        o_ref[...] = x_ref[...] + ff * _rms(ff, d_model) * g_post_ref[...]


def _ffn(x, g_pre, w_gate, w_up, w_down, g_post, *, tm, tf):
    m, d_model = x.shape
    d_ff = w_gate.shape[-1]
    n_blocks = pl.cdiv(d_ff, tf)
    tail = d_ff - (n_blocks - 1) * tf
    assert n_blocks >= 2 and tail % 256 == 0
    kern = functools.partial(_ffn_kernel, d_model=d_model, tf=tf, tail=tail)
    return pl.pallas_call(
        kern,
        grid=(m // tm, n_blocks),
        in_specs=[
            pl.BlockSpec((tm, d_model), lambda i, j: (i, 0)),
            pl.BlockSpec((1, d_model), lambda i, j: (0, 0)),
            pl.BlockSpec((d_model, tf), lambda i, j: (0, j)),
            pl.BlockSpec((d_model, tf), lambda i, j: (0, j)),
            pl.BlockSpec((tf, d_model), lambda i, j: (j, 0)),
            pl.BlockSpec((1, d_model), lambda i, j: (0, 0)),
        ],
        out_specs=pl.BlockSpec((tm, d_model), lambda i, j: (i, 0)),
        out_shape=jax.ShapeDtypeStruct(x.shape, x.dtype),
        scratch_shapes=[pltpu.VMEM((tm, d_model), jnp.bfloat16)],
        compiler_params=pltpu.CompilerParams(
            dimension_semantics=("arbitrary", "arbitrary"),
            vmem_limit_bytes=V7X_VMEM_LIMIT_BYTES),
        name="ffn",
    )(x, g_pre.reshape(1, -1), w_gate, w_up, w_down, g_post.reshape(1, -1))


def kernel(x, ln_mix_pre, w_in, conv_w, pool_w, pool_scale, w_out, ln_mix_post, ln_ffn_pre,
           w_gate, w_up, w_down, ln_ffn_post):
    bn, seq, d_model = x.shape
    depth = w_in.shape[0]
    bf = jnp.bfloat16
    for l in range(depth):
        x, wg, wu, wd = _mixer(x, ln_mix_pre[l], w_in[l].astype(bf), conv_w[l], pool_w[l].astype(bf),
                               pool_scale[l], w_out[l].astype(bf), ln_mix_post[l],
                               w_gate[l], w_up[l], w_down[l], ts=256)
        y = _ffn(x.reshape(bn * seq, d_model), ln_ffn_pre[l], wg, wu, wd, ln_ffn_post[l],
                 tm=512, tf=1024)
        x = y.reshape(bn, seq, d_model)
    return x
```

```python
import functools

import jax
import jax.numpy as jnp
from jax import lax
from jax.experimental import pallas as pl
from jax.experimental.pallas import tpu as pltpu

EPS = 1e-6
CONV_HEADS = 8
CONV_K = 3
POOL_WINDOWS = (2, 4, 8, 16)
HALO = 16

V7X_VMEM_LIMIT_BYTES = 62 * 1024 * 1024
FFN_DOWN_COLS = 512


def _rms(xf, axis_size):
    return lax.rsqrt(jnp.sum(xf * xf, axis=-1, keepdims=True) * (1.0 / axis_size) + EPS)


def _mixer_kernel(x_ref, xlag_ref, g_pre_ref, w_in_ref, conv_w_ref, pool_w_ref, pool_scale_ref,
                  w_out_ref, g_post_ref, wg_ref, wu_ref, wd_ref,
                  o_ref, wg_o, wu_o, wd_o,
                  cu_hist, v_hist, mixed_buf, *, ts, tiles_per_seq, n_tiles, d_model, cw, pw):
    t = pl.program_id(0)
    n_groups = len(POOL_WINDOWS)
    gd = pw // n_groups
    hd = cw // CONV_HEADS

    def back():
        mix_out = jnp.dot(mixed_buf[...], w_out_ref[...], preferred_element_type=jnp.float32)
        o_ref[0] = xlag_ref[0] + mix_out * _rms(mix_out, d_model) * g_post_ref[...]

    def front(first_step):
        s = lax.rem(t, tiles_per_seq)
        x = x_ref[0]
        h = (x * _rms(x, d_model) * g_pre_ref[...]).astype(jnp.bfloat16)

        def proj(lo, width):
            return jnp.dot(h, w_in_ref[:, lo:lo + width], preferred_element_type=jnp.float32)

        def history(hist_ref):
            if first_step:
                return jnp.zeros(hist_ref.shape, jnp.float32)
            return jnp.where(s == 0, 0.0, hist_ref[...])

        v = proj(3 * cw, pw)
        v_ext = jnp.concatenate([history(v_hist), v], axis=0)
        v_hist[...] = v[ts - HALO:, :]

        cu = proj(cw, cw) * proj(2 * cw, cw)
        cu_ext = jnp.concatenate([history(cu_hist), cu], axis=0)
        cu_hist[...] = cu[ts - HALO:, :]
        conv = conv_w_ref[CONV_K - 1:CONV_K, :] * cu
        for k in range(CONV_K - 1):
            delay = CONV_K - 1 - k
            conv = conv + conv_w_ref[k:k + 1, :] * pltpu.roll(cu_ext, delay, axis=0)[HALO:, :]
        y_conv = proj(0, cw) * conv
        for hh in range(CONV_HEADS):
            yh = y_conv[:, hh * hd:(hh + 1) * hd]
            mixed_buf[:, hh * hd:(hh + 1) * hd] = (yh * _rms(yh, hd)).astype(jnp.bfloat16)

        pos = s * ts + lax.broadcasted_iota(jnp.int32, (ts, 1), 0) + 1
        for gi, w in enumerate(POOL_WINDOWS):
            lo = gi * gd
            win = v_ext[:, lo:lo + gd]
            k = 1
            while k < w:
                win = win + (pltpu.roll(win, k, axis=0) if k % 8 else
                             jnp.concatenate([win[:k], win[:-k]], axis=0))
                k *= 2
            cnt = jnp.minimum(pos, w).astype(jnp.float32)
            pooled = (win[HALO:, :] / cnt - v[:, lo:lo + gd]).astype(jnp.bfloat16)
            yp = jnp.dot(pooled, pool_w_ref[gi], preferred_element_type=jnp.float32)
            yp = yp * _rms(yp, gd) * pool_scale_ref[:, lo:lo + gd]
            mixed_buf[:, cw + lo:cw + lo + gd] = yp.astype(jnp.bfloat16)

        wg_o[...] = wg_ref[...].astype(jnp.bfloat16)
        wu_o[...] = wu_ref[...].astype(jnp.bfloat16)
        wd_o[...] = wd_ref[...].astype(jnp.bfloat16)

    @pl.when(t == 0)
    def _():
        front(True)

    @pl.when(jnp.logical_and(t > 0, t < n_tiles))
    def _():
        back()
        front(False)

    @pl.when(t == n_tiles)
    def _():
        back()


def _mixer(x, g_pre, w_in, conv_w, pool_w, pool_scale, w_out, g_post, w_gate, w_up, w_down, *, ts):
    bn, seq, d_model = x.shape
    cw = conv_w.shape[-1]
    pw = pool_scale.shape[-1]
    d_ff = w_gate.shape[-1]
    tiles_per_seq = seq // ts
    n_tiles = bn * tiles_per_seq
    gu_rows = d_model // n_tiles
    wd_rows = 2 * d_ff // n_tiles
    assert gu_rows * n_tiles == d_model and gu_rows % 16 == 0
    assert wd_rows * n_tiles == 2 * d_ff and wd_rows % 16 == 0

    def const(shape):
        return pl.BlockSpec(shape, lambda t: (0,) * len(shape), pipeline_mode=pl.Buffered(1))

    def tile_map(lag):
        def index_map(t):
            i = jnp.clip(t - lag, 0, n_tiles - 1)
            return (i // tiles_per_seq, i % tiles_per_seq, 0)
        return index_map

    def gu_map(t):
        return (jnp.minimum(t, n_tiles - 1), 0)

    def wd_map(t):
        return (jnp.minimum(t, n_tiles - 1) // 2, 0)

    kern = functools.partial(_mixer_kernel, ts=ts, tiles_per_seq=tiles_per_seq, n_tiles=n_tiles,
                             d_model=d_model, cw=cw, pw=pw)
    bf = jnp.bfloat16
    return pl.pallas_call(
        kern,
        grid=(n_tiles + 1,),
        in_specs=[
            pl.BlockSpec((1, ts, d_model), tile_map(0)),
            pl.BlockSpec((1, ts, d_model), tile_map(1)),
            const((1, d_model)),
            const(w_in.shape),
            const(conv_w.shape),
            const(pool_w.shape),
            const((1, pw)),
            const(w_out.shape),
            const((1, d_model)),
            pl.BlockSpec((gu_rows, d_ff), gu_map),
            pl.BlockSpec((gu_rows, d_ff), gu_map),
            pl.BlockSpec((wd_rows, d_model), wd_map),
        ],
        out_specs=[
            pl.BlockSpec((1, ts, d_model), tile_map(1)),
            pl.BlockSpec((gu_rows, d_ff), gu_map),
            pl.BlockSpec((gu_rows, d_ff), gu_map),
            pl.BlockSpec((wd_rows, d_model), wd_map),
        ],
        out_shape=[
            jax.ShapeDtypeStruct(x.shape, x.dtype),
            jax.ShapeDtypeStruct(w_gate.shape, bf),
            jax.ShapeDtypeStruct(w_up.shape, bf),
            jax.ShapeDtypeStruct(w_down.shape, bf),
        ],
        scratch_shapes=[
            pltpu.VMEM((HALO, cw), jnp.float32),
            pltpu.VMEM((HALO, pw), jnp.float32),
            pltpu.VMEM((ts, cw + pw), jnp.bfloat16),
        ],
        compiler_params=pltpu.CompilerParams(
            dimension_semantics=("arbitrary",),
            vmem_limit_bytes=V7X_VMEM_LIMIT_BYTES),
        name="mixer",
    )(x, x, g_pre.reshape(1, -1), w_in, conv_w, pool_w, pool_scale.reshape(1, -1), w_out,
      g_post.reshape(1, -1), w_gate, w_up, w_down)


def _ffn_kernel(x_ref, g_pre_ref, wg_ref, wu_ref, wd_ref, g_post_ref, o_ref, hf_buf,
                *, d_model, tf, tail):
    j = pl.program_id(1)
    last = pl.num_programs(1) - 1

    def partial_ffn(hf, width):
        g = jnp.dot(hf, wg_ref[:, :width], preferred_element_type=jnp.float32)
        up = jnp.dot(hf, wu_ref[:, :width], preferred_element_type=jnp.float32)
        act = (g * jax.nn.sigmoid(g) * up).astype(jnp.bfloat16)
        return [(n, jnp.dot(act, wd_ref[:width, n:n + FFN_DOWN_COLS],
                            preferred_element_type=jnp.float32))
                for n in range(0, d_model, FFN_DOWN_COLS)]

    @pl.when(j == 0)
    def _():
        x = x_ref[...]
        hf = (x * _rms(x, d_model) * g_pre_ref[...]).astype(jnp.bfloat16)
        hf_buf[...] = hf
        for n, part in partial_ffn(hf, tf):
            o_ref[:, n:n + FFN_DOWN_COLS] = part

    @pl.when(jnp.logical_and(j > 0, j < last))
    def _():
        for n, part in partial_ffn(hf_buf[...], tf):
            o_ref[:, n:n + FFN_DOWN_COLS] += part

    @pl.when(j == last)
    def _():
        parts = [part for _, part in partial_ffn(hf_buf[...], tail)]
        ff = o_ref[...] + jnp.concatenate(parts, axis=1)
        o_ref[...] = x_ref[...] + ff * _rms(ff, d_model) * g_post_ref[...]


def _ffn(x, g_pre, w_gate, w_up, w_down, g_post, *, tm, tf):
    m, d_model = x.shape
    d_ff = w_gate.shape[-1]
    n_blocks = pl.cdiv(d_ff, tf)
    tail = d_ff - (n_blocks - 1) * tf
    assert n_blocks >= 2 and tail % 256 == 0
    kern = functools.partial(_ffn_kernel, d_model=d_model, tf=tf, tail=tail)
    return pl.pallas_call(
        kern,
        grid=(m // tm, n_blocks),
        in_specs=[
            pl.BlockSpec((tm, d_model), lambda i, j: (i, 0)),
            pl.BlockSpec((1, d_model), lambda i, j: (0, 0)),
            pl.BlockSpec((d_model, tf), lambda i, j: (0, j)),
            pl.BlockSpec((d_model, tf), lambda i, j: (0, j)),
            pl.BlockSpec((tf, d_model), lambda i, j: (j, 0)),
            pl.BlockSpec((1, d_model), lambda i, j: (0, 0)),
        ],
        out_specs=pl.BlockSpec((tm, d_model), lambda i, j: (i, 0)),
        out_shape=jax.ShapeDtypeStruct(x.shape, x.dtype),
        scratch_shapes=[pltpu.VMEM((tm, d_model), jnp.bfloat16)],
        compiler_params=pltpu.CompilerParams(
            dimension_semantics=("arbitrary", "arbitrary"),
            vmem_limit_bytes=V7X_VMEM_LIMIT_BYTES),
        name="ffn",
    )(x, g_pre.reshape(1, -1), w_gate, w_up, w_down, g_post.reshape(1, -1))


def kernel(x, ln_mix_pre, w_in, conv_w, pool_w, pool_scale, w_out, ln_mix_post, ln_ffn_pre,
           w_gate, w_up, w_down, ln_ffn_post):
    bn, seq, d_model = x.shape
    depth = w_in.shape[0]
    bf = jnp.bfloat16
    for l in range(depth):
        x, wg, wu, wd = _mixer(x, ln_mix_pre[l], w_in[l].astype(bf), conv_w[l], pool_w[l].astype(bf),
                               pool_scale[l], w_out[l].astype(bf), ln_mix_post[l],
                               w_gate[l], w_up[l], w_down[l], ts=256)
        y = _ffn(x.reshape(bn * seq, d_model), ln_ffn_pre[l], wg, wu, wd, ln_ffn_post[l],
                 tm=1024, tf=512)
        x = y.reshape(bn, seq, d_model)
    return x
```

```python
import functools

import jax
import jax.numpy as jnp
from jax import lax
from jax.experimental import pallas as pl
from jax.experimental.pallas import tpu as pltpu

EPS = 1e-6
CONV_HEADS = 8
CONV_K = 3
POOL_WINDOWS = (2, 4, 8, 16)
HALO = 16

V7X_VMEM_LIMIT_BYTES = 62 * 1024 * 1024
FFN_DOWN_COLS = 512


def _rms(xf, axis_size):
    return lax.rsqrt(jnp.sum(xf * xf, axis=-1, keepdims=True) * (1.0 / axis_size) + EPS)


def _mixer_kernel(x_ref, g_pre_ref, w_in_ref, conv_w_ref, pool_w_ref, pool_scale_ref,
                  w_out_ref, g_post_ref, wg_ref, wu_ref, wd_ref,
                  o_ref, wg_o, wu_o, wd_o,
                  cu_hist, v_hist, mixed_buf, *, ts, d_model, cw, pw):
    s = pl.program_id(1)
    n_groups = len(POOL_WINDOWS)
    gd = pw // n_groups
    hd = cw // CONV_HEADS

    @pl.when(s == 0)
    def _():
        cu_hist[...] = jnp.zeros_like(cu_hist)
        v_hist[...] = jnp.zeros_like(v_hist)

    x = x_ref[0]
    h = (x * _rms(x, d_model) * g_pre_ref[...]).astype(jnp.bfloat16)

    def proj(lo, width):
        return jnp.dot(h, w_in_ref[:, lo:lo + width], preferred_element_type=jnp.float32)

    v = proj(3 * cw, pw)
    v_ext = jnp.concatenate([v_hist[...], v], axis=0)
    v_hist[...] = v[ts - HALO:, :]

    cu = proj(cw, cw) * proj(2 * cw, cw)
    cu_ext = jnp.concatenate([cu_hist[...], cu], axis=0)
    cu_hist[...] = cu[ts - HALO:, :]
    conv = conv_w_ref[CONV_K - 1:CONV_K, :] * cu
    for k in range(CONV_K - 1):
        delay = CONV_K - 1 - k
        conv = conv + conv_w_ref[k:k + 1, :] * pltpu.roll(cu_ext, delay, axis=0)[HALO:, :]
    y_conv = proj(0, cw) * conv
    for hh in range(CONV_HEADS):
        yh = y_conv[:, hh * hd:(hh + 1) * hd]
        mixed_buf[:, hh * hd:(hh + 1) * hd] = (yh * _rms(yh, hd)).astype(jnp.bfloat16)

    pos = s * ts + lax.broadcasted_iota(jnp.int32, (ts, 1), 0) + 1
    for gi, w in enumerate(POOL_WINDOWS):
        lo = gi * gd
        win = v_ext[:, lo:lo + gd]
        k = 1
        while k < w:
            win = win + (pltpu.roll(win, k, axis=0) if k % 8 else
                         jnp.concatenate([win[:k], win[:-k]], axis=0))
            k *= 2
        cnt = jnp.minimum(pos, w).astype(jnp.float32)
        pooled = (win[HALO:, :] / cnt - v[:, lo:lo + gd]).astype(jnp.bfloat16)
        yp = jnp.dot(pooled, pool_w_ref[gi], preferred_element_type=jnp.float32)
        yp = yp * _rms(yp, gd) * pool_scale_ref[:, lo:lo + gd]
        mixed_buf[:, cw + lo:cw + lo + gd] = yp.astype(jnp.bfloat16)

    mix_out = jnp.dot(mixed_buf[...], w_out_ref[...], preferred_element_type=jnp.float32)
    o_ref[0] = x + mix_out * _rms(mix_out, d_model) * g_post_ref[...]

    wg_o[...] = wg_ref[...].astype(jnp.bfloat16)
    wu_o[...] = wu_ref[...].astype(jnp.bfloat16)
    wd_o[...] = wd_ref[...].astype(jnp.bfloat16)


def _mixer(x, g_pre, w_in, conv_w, pool_w, pool_scale, w_out, g_post, w_gate, w_up, w_down, *, ts):
    bn, seq, d_model = x.shape
    cw = conv_w.shape[-1]
    pw = pool_scale.shape[-1]
    d_ff = w_gate.shape[-1]
    tiles_per_seq = seq // ts
    n_tiles = bn * tiles_per_seq
    gu_rows = d_model // n_tiles
    wd_rows = 2 * d_ff // n_tiles
    assert gu_rows * n_tiles == d_model and gu_rows % 16 == 0
    assert wd_rows * n_tiles == 2 * d_ff and wd_rows % 16 == 0

    def const(shape):
        return pl.BlockSpec(shape, lambda b, s: (0,) * len(shape), pipeline_mode=pl.Buffered(1))

    def gu_map(b, s):
        return (b * tiles_per_seq + s, 0)

    def wd_map(b, s):
        return ((b * tiles_per_seq + s) // 2, 0)

    kern = functools.partial(_mixer_kernel, ts=ts, d_model=d_model, cw=cw, pw=pw)
    bf = jnp.bfloat16
    return pl.pallas_call(
        kern,
        grid=(bn, tiles_per_seq),
        in_specs=[
            pl.BlockSpec((1, ts, d_model), lambda b, s: (b, s, 0)),
            const((1, d_model)),
            const(w_in.shape),
            const(conv_w.shape),
            const(pool_w.shape),
            const((1, pw)),
            const(w_out.shape),
            const((1, d_model)),
            pl.BlockSpec((gu_rows, d_ff), gu_map),
            pl.BlockSpec((gu_rows, d_ff), gu_map),
            pl.BlockSpec((wd_rows, d_model), wd_map),
        ],
        out_specs=[
            pl.BlockSpec((1, ts, d_model), lambda b, s: (b, s, 0)),
            pl.BlockSpec((gu_rows, d_ff), gu_map),
            pl.BlockSpec((gu_rows, d_ff), gu_map),
            pl.BlockSpec((wd_rows, d_model), wd_map),
        ],
        out_shape=[
            jax.ShapeDtypeStruct(x.shape, x.dtype),
            jax.ShapeDtypeStruct(w_gate.shape, bf),
            jax.ShapeDtypeStruct(w_up.shape, bf),
            jax.ShapeDtypeStruct(w_down.shape, bf),
        ],
        scratch_shapes=[
            pltpu.VMEM((HALO, cw), jnp.float32),
            pltpu.VMEM((HALO, pw), jnp.float32),
            pltpu.VMEM((ts, cw + pw), jnp.bfloat16),
        ],
        compiler_params=pltpu.CompilerParams(
            dimension_semantics=("arbitrary", "arbitrary"),
            vmem_limit_bytes=V7X_VMEM_LIMIT_BYTES),
        name="mixer",
    )(x, g_pre.reshape(1, -1), w_in, conv_w, pool_w, pool_scale.reshape(1, -1), w_out,
      g_post.reshape(1, -1), w_gate, w_up, w_down)


def _ffn_kernel(x_ref, g_pre_ref, wg_ref, wu_ref, wd_ref, g_post_ref, o_ref, hf_buf,
                *, d_model, tf, tail):
    j = pl.program_id(1)
    last = pl.num_programs(1) - 1

    def partial_ffn(hf, width):
        g = jnp.dot(hf, wg_ref[:, :width], preferred_element_type=jnp.float32)
        up = jnp.dot(hf, wu_ref[:, :width], preferred_element_type=jnp.float32)
        act = (g * jax.nn.sigmoid(g) * up).astype(jnp.bfloat16)
        return [(n, jnp.dot(act, wd_ref[:width, n:n + FFN_DOWN_COLS],
                            preferred_element_type=jnp.float32))
                for n in range(0, d_model, FFN_DOWN_COLS)]

    @pl.when(j == 0)
    def _():
        x = x_ref[...]
        hf = (x * _rms(x, d_model) * g_pre_ref[...]).astype(jnp.bfloat16)
        hf_buf[...] = hf
        for n, part in partial_ffn(hf, tf):
            o_ref[:, n:n + FFN_DOWN_COLS] = part

    @pl.when(jnp.logical_and(j > 0, j < last))
    def _():
        for n, part in partial_ffn(hf_buf[...], tf):
            o_ref[:, n:n + FFN_DOWN_COLS] += part

    @pl.when(j == last)
    def _():
        parts = [part for _, part in partial_ffn(hf_buf[...], tail)]
        ff = o_ref[...] + jnp.concatenate(parts, axis=1)
        o_ref[...] = x_ref[...] + ff * _rms(ff, d_model) * g_post_ref[...]


def _ffn(x, g_pre, w_gate, w_up, w_down, g_post, *, tm, tf):
    m, d_model = x.shape
    d_ff = w_gate.shape[-1]
    n_blocks = pl.cdiv(d_ff, tf)
    tail = d_ff - (n_blocks - 1) * tf
    assert n_blocks >= 2 and tail % 256 == 0
    kern = functools.partial(_ffn_kernel, d_model=d_model, tf=tf, tail=tail)
    return pl.pallas_call(
        kern,
        grid=(m // tm, n_blocks),
        in_specs=[
            pl.BlockSpec((tm, d_model), lambda i, j: (i, 0)),
            pl.BlockSpec((1, d_model), lambda i, j: (0, 0)),
            pl.BlockSpec((d_model, tf), lambda i, j: (0, j)),
            pl.BlockSpec((d_model, tf), lambda i, j: (0, j)),
            pl.BlockSpec((tf, d_model), lambda i, j: (j, 0)),
            pl.BlockSpec((1, d_model), lambda i, j: (0, 0)),
        ],
        out_specs=pl.BlockSpec((tm, d_model), lambda i, j: (i, 0)),
        out_shape=jax.ShapeDtypeStruct(x.shape, x.dtype),
        scratch_shapes=[pltpu.VMEM((tm, d_model), jnp.bfloat16)],
        compiler_params=pltpu.CompilerParams(
            dimension_semantics=("arbitrary", "arbitrary"),
            vmem_limit_bytes=V7X_VMEM_LIMIT_BYTES),
        name="ffn",
    )(x, g_pre.reshape(1, -1), w_gate, w_up, w_down, g_post.reshape(1, -1))


def kernel(x, ln_mix_pre, w_in, conv_w, pool_w, pool_scale, w_out, ln_mix_post, ln_ffn_pre,
           w_gate, w_up, w_down, ln_ffn_post):
    bn, seq, d_model = x.shape
    depth = w_in.shape[0]
    bf = jnp.bfloat16
    for l in range(depth):
        x, wg, wu, wd = _mixer(x, ln_mix_pre[l], w_in[l].astype(bf), conv_w[l], pool_w[l].astype(bf),
                               pool_scale[l], w_out[l].astype(bf), ln_mix_post[l],
                               w_gate[l], w_up[l], w_down[l], ts=256)
        y = _ffn(x.reshape(bn * seq, d_model), ln_ffn_pre[l], wg, wu, wd, ln_ffn_post[l],
                 tm=1024, tf=512)
        x = y.reshape(bn, seq, d_model)
    return x
```

```python
import functools

import jax
import jax.numpy as jnp
from jax import lax
from jax.experimental import pallas as pl
from jax.experimental.pallas import tpu as pltpu

EPS = 1e-6
CONV_HEADS = 8
CONV_K = 3
POOL_WINDOWS = (2, 4, 8, 16)
HALO = 16

V7X_VMEM_LIMIT_BYTES = 62 * 1024 * 1024
WEIGHT_SLAB_COLS = 512

MIXER_SEQ_TILE = 256
FFN_ROW_TILE = 1024
FFN_DFF_BLOCK = 512


def _slab_shape(shape):
    k, n = shape
    return (n // WEIGHT_SLAB_COLS, k, WEIGHT_SLAB_COLS)


def _col_slabs(w):
    k, n = w.shape
    return w.reshape(k, n // WEIGHT_SLAB_COLS, WEIGHT_SLAB_COLS).transpose(1, 0, 2)


def _rms(xf, axis_size):
    return lax.rsqrt(jnp.sum(xf * xf, axis=-1, keepdims=True) * (1.0 / axis_size) + EPS)


def _mixer_kernel(x_ref, g_pre_ref, w_in_ref, conv_w_ref, pool_w_ref, pool_scale_ref,
                  w_out_ref, g_post_ref, wg_ref, wu_ref, wd_ref,
                  o_ref, wg_o, wu_o, wd_o,
                  cu_hist, v_hist, mixed_buf, *, ts, d_model, cw, pw):
    s = pl.program_id(0)
    n_groups = len(POOL_WINDOWS)
    gd = pw // n_groups
    hd = cw // CONV_HEADS

    @pl.when(s == 0)
    def _():
        cu_hist[...] = jnp.zeros_like(cu_hist)
        v_hist[...] = jnp.zeros_like(v_hist)

    pos = s * ts + lax.broadcasted_iota(jnp.int32, (ts, 1), 0) + 1

    def one_sequence(b):
        x = x_ref[b]
        h = (x * _rms(x, d_model) * g_pre_ref[...]).astype(jnp.bfloat16)

        def proj(lo, width):
            return jnp.concatenate(
                [jnp.dot(h, w_in_ref[c], preferred_element_type=jnp.float32)
                 for c in range(lo // WEIGHT_SLAB_COLS, (lo + width) // WEIGHT_SLAB_COLS)], axis=1)

        v = proj(3 * cw, pw)
        v_ext = jnp.concatenate([v_hist[b], v], axis=0)
        v_hist[b] = v[ts - HALO:, :]

        cu = proj(cw, cw) * proj(2 * cw, cw)
        cu_ext = jnp.concatenate([cu_hist[b], cu], axis=0)
        cu_hist[b] = cu[ts - HALO:, :]
        conv = conv_w_ref[CONV_K - 1:CONV_K, :] * cu
        for k in range(CONV_K - 1):
            delay = CONV_K - 1 - k
            conv = conv + conv_w_ref[k:k + 1, :] * pltpu.roll(cu_ext, delay, axis=0)[HALO:, :]
        y_conv = proj(0, cw) * conv
        for hh in range(CONV_HEADS):
            yh = y_conv[:, hh * hd:(hh + 1) * hd]
            mixed_buf[b, :, hh * hd:(hh + 1) * hd] = (yh * _rms(yh, hd)).astype(jnp.bfloat16)

        for gi, w in enumerate(POOL_WINDOWS):
            lo = gi * gd
            win = v_ext[:, lo:lo + gd]
            k = 1
            while k < w:
                win = win + (pltpu.roll(win, k, axis=0) if k % 8 else
                             jnp.concatenate([win[:k], win[:-k]], axis=0))
                k *= 2
            cnt = jnp.minimum(pos, w).astype(jnp.float32)
            pooled = (win[HALO:, :] / cnt - v[:, lo:lo + gd]).astype(jnp.bfloat16)
            yp = jnp.dot(pooled, pool_w_ref[gi], preferred_element_type=jnp.float32)
            yp = yp * _rms(yp, gd) * pool_scale_ref[:, lo:lo + gd]
            mixed_buf[b, :, cw + lo:cw + lo + gd] = yp.astype(jnp.bfloat16)

        mixed = mixed_buf[b]
        mix_out = jnp.concatenate(
            [jnp.dot(mixed, w_out_ref[c], preferred_element_type=jnp.float32)
             for c in range(w_out_ref.shape[0])], axis=1)
        o_ref[b] = x + mix_out * _rms(mix_out, d_model) * g_post_ref[...]

    for b in range(x_ref.shape[0]):
        one_sequence(b)

    wg_o[...] = wg_ref[...].astype(jnp.bfloat16)
    wu_o[...] = wu_ref[...].astype(jnp.bfloat16)
    for c in range(wd_o.shape[0]):
        wd_o[c] = wd_ref[:, c * WEIGHT_SLAB_COLS:(c + 1) * WEIGHT_SLAB_COLS].astype(jnp.bfloat16)


def _mixer(x, g_pre, w_in, conv_w, pool_w, pool_scale, w_out, g_post, w_gate, w_up, w_down):
    bn, seq, d_model = x.shape
    ts = MIXER_SEQ_TILE
    cw = conv_w.shape[-1]
    pw = pool_scale.shape[-1]
    d_ff = w_gate.shape[-1]
    assert seq % ts == 0
    n_steps = seq // ts
    gu_rows = d_model // n_steps
    wd_rows = d_ff // n_steps
    assert gu_rows * n_steps == d_model and gu_rows % 16 == 0
    assert wd_rows * n_steps == d_ff and wd_rows % 16 == 0

    def const(shape):
        return pl.BlockSpec(shape, lambda s: (0,) * len(shape), pipeline_mode=pl.Buffered(1))

    def rows_map(s):
        return (s, 0)

    kern = functools.partial(_mixer_kernel, ts=ts, d_model=d_model, cw=cw, pw=pw)
    bf = jnp.bfloat16
    return pl.pallas_call(
        kern,
        grid=(n_steps,),
        in_specs=[
            pl.BlockSpec((bn, ts, d_model), lambda s: (0, s, 0)),
            const((1, d_model)),
            const(_slab_shape(w_in.shape)),
            const(conv_w.shape),
            const(pool_w.shape),
            const((1, pw)),
            const(_slab_shape(w_out.shape)),
            const((1, d_model)),
            pl.BlockSpec((gu_rows, d_ff), rows_map),
            pl.BlockSpec((gu_rows, d_ff), rows_map),
            pl.BlockSpec((wd_rows, d_model), rows_map),
        ],
        out_specs=[
            pl.BlockSpec((bn, ts, d_model), lambda s: (0, s, 0)),
            pl.BlockSpec((gu_rows, d_ff), rows_map),
            pl.BlockSpec((gu_rows, d_ff), rows_map),
            pl.BlockSpec((d_model // WEIGHT_SLAB_COLS, wd_rows, WEIGHT_SLAB_COLS), lambda s: (0, s, 0)),
        ],
        out_shape=[
            jax.ShapeDtypeStruct(x.shape, x.dtype),
            jax.ShapeDtypeStruct(w_gate.shape, bf),
            jax.ShapeDtypeStruct(w_up.shape, bf),
            jax.ShapeDtypeStruct(_slab_shape(w_down.shape), bf),
        ],
        scratch_shapes=[
            pltpu.VMEM((bn, HALO, cw), jnp.float32),
            pltpu.VMEM((bn, HALO, pw), jnp.float32),
            pltpu.VMEM((bn, ts, cw + pw), jnp.bfloat16),
        ],
        compiler_params=pltpu.CompilerParams(
            dimension_semantics=("arbitrary",),
            vmem_limit_bytes=V7X_VMEM_LIMIT_BYTES),
        name="mixer",
    )(x, g_pre.reshape(1, -1), _col_slabs(w_in), conv_w, pool_w, pool_scale.reshape(1, -1),
      _col_slabs(w_out), g_post.reshape(1, -1), w_gate, w_up, w_down)


def _ffn_kernel(x_ref, g_pre_ref, wg_ref, wu_ref, wd_ref, g_post_ref, o_ref, hf_buf,
                *, d_model, tf, tail):
    j = pl.program_id(1)
    last = pl.num_programs(1) - 1

    def partial_ffn(hf, width):
        g = jnp.dot(hf, wg_ref[:, :width], preferred_element_type=jnp.float32)
        up = jnp.dot(hf, wu_ref[:, :width], preferred_element_type=jnp.float32)
        act = (g * jax.nn.sigmoid(g) * up).astype(jnp.bfloat16)
        return [(c * WEIGHT_SLAB_COLS,
                 jnp.dot(act, wd_ref[c, :width, :], preferred_element_type=jnp.float32))
                for c in range(wd_ref.shape[0])]

    @pl.when(j == 0)
    def _():
        x = x_ref[...]
        hf = (x * _rms(x, d_model) * g_pre_ref[...]).astype(jnp.bfloat16)
        hf_buf[...] = hf
        for n, part in partial_ffn(hf, tf):
            o_ref[:, n:n + WEIGHT_SLAB_COLS] = part

    @pl.when(jnp.logical_and(j > 0, j < last))
    def _():
        for n, part in partial_ffn(hf_buf[...], tf):
            o_ref[:, n:n + WEIGHT_SLAB_COLS] += part

    @pl.when(j == last)
    def _():
        parts = [part for _, part in partial_ffn(hf_buf[...], tail)]
        ff = o_ref[...] + jnp.concatenate(parts, axis=1)
        o_ref[...] = x_ref[...] + ff * _rms(ff, d_model) * g_post_ref[...]


def _ffn(x, g_pre, w_gate, w_up, w_down, g_post):
    m, d_model = x.shape
    tm, tf = FFN_ROW_TILE, FFN_DFF_BLOCK
    d_ff = w_gate.shape[-1]
    n_blocks = pl.cdiv(d_ff, tf)
    tail = d_ff - (n_blocks - 1) * tf
    assert m % tm == 0 and n_blocks >= 2 and tail % 256 == 0
    kern = functools.partial(_ffn_kernel, d_model=d_model, tf=tf, tail=tail)
    return pl.pallas_call(
        kern,
        grid=(m // tm, n_blocks),
        in_specs=[
            pl.BlockSpec((tm, d_model), lambda i, j: (i, 0)),
            pl.BlockSpec((1, d_model), lambda i, j: (0, 0)),
            pl.BlockSpec((d_model, tf), lambda i, j: (0, j)),
            pl.BlockSpec((d_model, tf), lambda i, j: (0, j)),
            pl.BlockSpec((d_model // WEIGHT_SLAB_COLS, tf, WEIGHT_SLAB_COLS), lambda i, j: (0, j, 0)),
            pl.BlockSpec((1, d_model), lambda i, j: (0, 0)),
        ],
        out_specs=pl.BlockSpec((tm, d_model), lambda i, j: (i, 0)),
        out_shape=jax.ShapeDtypeStruct(x.shape, x.dtype),
        scratch_shapes=[pltpu.VMEM((tm, d_model), jnp.bfloat16)],
        compiler_params=pltpu.CompilerParams(
            dimension_semantics=("arbitrary", "arbitrary"),
            vmem_limit_bytes=V7X_VMEM_LIMIT_BYTES),
        name="ffn",
    )(x, g_pre.reshape(1, -1), w_gate, w_up, w_down, g_post.reshape(1, -1))


def kernel(x, ln_mix_pre, w_in, conv_w, pool_w, pool_scale, w_out, ln_mix_post, ln_ffn_pre,
           w_gate, w_up, w_down, ln_ffn_post):
    bn, seq, d_model = x.shape
    depth = w_in.shape[0]
    bf = jnp.bfloat16
    for l in range(depth):
        x, wg, wu, wd = _mixer(x, ln_mix_pre[l], w_in[l].astype(bf), conv_w[l], pool_w[l].astype(bf),
                               pool_scale[l], w_out[l].astype(bf), ln_mix_post[l],
                               w_gate[l], w_up[l], w_down[l])
        y = _ffn(x.reshape(bn * seq, d_model), ln_ffn_pre[l], wg, wu, wd, ln_ffn_post[l])
        x = y.reshape(bn, seq, d_model)
    return x
```

```python
import functools

import jax
import jax.numpy as jnp
from jax import lax
from jax.experimental import pallas as pl
from jax.experimental.pallas import tpu as pltpu

EPS = 1e-6
CONV_HEADS = 8
CONV_K = 3
POOL_WINDOWS = (2, 4, 8, 16)
HALO = 16

V7X_VMEM_LIMIT_BYTES = 62 * 1024 * 1024
FFN_DOWN_COLS = 512

MIXER_SEQ_TILE = 256
FFN_ROW_TILE = 1024
FFN_DFF_BLOCK = 512


def _rms(xf, axis_size):
    return lax.rsqrt(jnp.sum(xf * xf, axis=-1, keepdims=True) * (1.0 / axis_size) + EPS)


def _mixer_kernel(x_ref, g_pre_ref, w_in_ref, conv_w_ref, pool_w_ref, pool_scale_ref,
                  w_out_ref, g_post_ref, wg_ref, wu_ref, wd_ref,
                  o_ref, wg_o, wu_o, wd_o,
                  cu_hist, v_hist, mixed_buf, *, ts, d_model, cw, pw):
    s = pl.program_id(0)
    n_groups = len(POOL_WINDOWS)
    gd = pw // n_groups
    hd = cw // CONV_HEADS

    @pl.when(s == 0)
    def _():
        cu_hist[...] = jnp.zeros_like(cu_hist)
        v_hist[...] = jnp.zeros_like(v_hist)

    pos = s * ts + lax.broadcasted_iota(jnp.int32, (ts, 1), 0) + 1

    def one_sequence(b):
        x = x_ref[b]
        h = (x * _rms(x, d_model) * g_pre_ref[...]).astype(jnp.bfloat16)

        def proj(lo, width):
            return jnp.dot(h, w_in_ref[:, lo:lo + width], preferred_element_type=jnp.float32)

        v = proj(3 * cw, pw)
        v_ext = jnp.concatenate([v_hist[b], v], axis=0)
        v_hist[b] = v[ts - HALO:, :]

        cu = proj(cw, cw) * proj(2 * cw, cw)
        cu_ext = jnp.concatenate([cu_hist[b], cu], axis=0)
        cu_hist[b] = cu[ts - HALO:, :]
        conv = conv_w_ref[CONV_K - 1:CONV_K, :] * cu
        for k in range(CONV_K - 1):
            delay = CONV_K - 1 - k
            conv = conv + conv_w_ref[k:k + 1, :] * pltpu.roll(cu_ext, delay, axis=0)[HALO:, :]
        y_conv = proj(0, cw) * conv
        for hh in range(CONV_HEADS):
            yh = y_conv[:, hh * hd:(hh + 1) * hd]
            mixed_buf[b, :, hh * hd:(hh + 1) * hd] = (yh * _rms(yh, hd)).astype(jnp.bfloat16)

        for gi, w in enumerate(POOL_WINDOWS):
            lo = gi * gd
            win = v_ext[:, lo:lo + gd]
            k = 1
            while k < w:
                win = win + (pltpu.roll(win, k, axis=0) if k % 8 else
                             jnp.concatenate([win[:k], win[:-k]], axis=0))
                k *= 2
            cnt = jnp.minimum(pos, w).astype(jnp.float32)
            pooled = (win[HALO:, :] / cnt - v[:, lo:lo + gd]).astype(jnp.bfloat16)
            yp = jnp.dot(pooled, pool_w_ref[gi], preferred_element_type=jnp.float32)
            yp = yp * _rms(yp, gd) * pool_scale_ref[:, lo:lo + gd]
            mixed_buf[b, :, cw + lo:cw + lo + gd] = yp.astype(jnp.bfloat16)

        mix_out = jnp.dot(mixed_buf[b], w_out_ref[...], preferred_element_type=jnp.float32)
        o_ref[b] = x + mix_out * _rms(mix_out, d_model) * g_post_ref[...]

    for b in range(x_ref.shape[0]):
        one_sequence(b)

    wg_o[...] = wg_ref[...].astype(jnp.bfloat16)
    wu_o[...] = wu_ref[...].astype(jnp.bfloat16)
    wd_o[...] = wd_ref[...].astype(jnp.bfloat16)


def _mixer(x, g_pre, w_in, conv_w, pool_w, pool_scale, w_out, g_post, w_gate, w_up, w_down):
    bn, seq, d_model = x.shape
    ts = MIXER_SEQ_TILE
    cw = conv_w.shape[-1]
    pw = pool_scale.shape[-1]
    d_ff = w_gate.shape[-1]
    assert seq % ts == 0
    n_steps = seq // ts
    gu_rows = d_model // n_steps
    wd_rows = d_ff // n_steps
    assert gu_rows * n_steps == d_model and gu_rows % 16 == 0
    assert wd_rows * n_steps == d_ff and wd_rows % 16 == 0

    def const(shape):
        return pl.BlockSpec(shape, lambda s: (0,) * len(shape), pipeline_mode=pl.Buffered(1))

    def rows_map(s):
        return (s, 0)

    kern = functools.partial(_mixer_kernel, ts=ts, d_model=d_model, cw=cw, pw=pw)
    bf = jnp.bfloat16
    return pl.pallas_call(
        kern,
        grid=(n_steps,),
        in_specs=[
            pl.BlockSpec((bn, ts, d_model), lambda s: (0, s, 0)),
            const((1, d_model)),
            const(w_in.shape),
            const(conv_w.shape),
            const(pool_w.shape),
            const((1, pw)),
            const(w_out.shape),
            const((1, d_model)),
            pl.BlockSpec((gu_rows, d_ff), rows_map),
            pl.BlockSpec((gu_rows, d_ff), rows_map),
            pl.BlockSpec((wd_rows, d_model), rows_map),
        ],
        out_specs=[
            pl.BlockSpec((bn, ts, d_model), lambda s: (0, s, 0)),
            pl.BlockSpec((gu_rows, d_ff), rows_map),
            pl.BlockSpec((gu_rows, d_ff), rows_map),
            pl.BlockSpec((wd_rows, d_model), rows_map),
        ],
        out_shape=[
            jax.ShapeDtypeStruct(x.shape, x.dtype),
            jax.ShapeDtypeStruct(w_gate.shape, bf),
            jax.ShapeDtypeStruct(w_up.shape, bf),
            jax.ShapeDtypeStruct(w_down.shape, bf),
        ],
        scratch_shapes=[
            pltpu.VMEM((bn, HALO, cw), jnp.float32),
            pltpu.VMEM((bn, HALO, pw), jnp.float32),
            pltpu.VMEM((bn, ts, cw + pw), jnp.bfloat16),
        ],
        compiler_params=pltpu.CompilerParams(
            dimension_semantics=("arbitrary",),
            vmem_limit_bytes=V7X_VMEM_LIMIT_BYTES),
        name="mixer",
    )(x, g_pre.reshape(1, -1), w_in, conv_w, pool_w, pool_scale.reshape(1, -1), w_out,
      g_post.reshape(1, -1), w_gate, w_up, w_down)


def _ffn_kernel(x_ref, g_pre_ref, wg_ref, wu_ref, wd_ref, g_post_ref, o_ref, hf_buf,
                *, d_model, tf, tail):
    j = pl.program_id(1)
    last = pl.num_programs(1) - 1

    def partial_ffn(hf, width):
        g = jnp.dot(hf, wg_ref[:, :width], preferred_element_type=jnp.float32)
        up = jnp.dot(hf, wu_ref[:, :width], preferred_element_type=jnp.float32)
        act = (g * jax.nn.sigmoid(g) * up).astype(jnp.bfloat16)
        return [(n, jnp.dot(act, wd_ref[:width, n:n + FFN_DOWN_COLS],
                            preferred_element_type=jnp.float32))
                for n in range(0, d_model, FFN_DOWN_COLS)]

    @pl.when(j == 0)
    def _():
        x = x_ref[...]
        hf = (x * _rms(x, d_model) * g_pre_ref[...]).astype(jnp.bfloat16)
        hf_buf[...] = hf
        for n, part in partial_ffn(hf, tf):
            o_ref[:, n:n + FFN_DOWN_COLS] = part

    @pl.when(jnp.logical_and(j > 0, j < last))
    def _():
        for n, part in partial_ffn(hf_buf[...], tf):
            o_ref[:, n:n + FFN_DOWN_COLS] += part

    @pl.when(j == last)
    def _():
        parts = [part for _, part in partial_ffn(hf_buf[...], tail)]
        ff = o_ref[...] + jnp.concatenate(parts, axis=1)
        o_ref[...] = x_ref[...] + ff * _rms(ff, d_model) * g_post_ref[...]


def _ffn(x, g_pre, w_gate, w_up, w_down, g_post):
    m, d_model = x.shape
    tm, tf = FFN_ROW_TILE, FFN_DFF_BLOCK
    d_ff = w_gate.shape[-1]
    n_blocks = pl.cdiv(d_ff, tf)
    tail = d_ff - (n_blocks - 1) * tf
    assert m % tm == 0 and n_blocks >= 2 and tail % 256 == 0
    kern = functools.partial(_ffn_kernel, d_model=d_model, tf=tf, tail=tail)
    return pl.pallas_call(
        kern,
        grid=(m // tm, n_blocks),
        in_specs=[
            pl.BlockSpec((tm, d_model), lambda i, j: (i, 0)),
            pl.BlockSpec((1, d_model), lambda i, j: (0, 0)),
            pl.BlockSpec((d_model, tf), lambda i, j: (0, j)),
            pl.BlockSpec((d_model, tf), lambda i, j: (0, j)),
            pl.BlockSpec((tf, d_model), lambda i, j: (j, 0)),
            pl.BlockSpec((1, d_model), lambda i, j: (0, 0)),
        ],
        out_specs=pl.BlockSpec((tm, d_model), lambda i, j: (i, 0)),
        out_shape=jax.ShapeDtypeStruct(x.shape, x.dtype),
        scratch_shapes=[pltpu.VMEM((tm, d_model), jnp.bfloat16)],
        compiler_params=pltpu.CompilerParams(
            dimension_semantics=("arbitrary", "arbitrary"),
            vmem_limit_bytes=V7X_VMEM_LIMIT_BYTES),
        name="ffn",
    )(x, g_pre.reshape(1, -1), w_gate, w_up, w_down, g_post.reshape(1, -1))


def kernel(x, ln_mix_pre, w_in, conv_w, pool_w, pool_scale, w_out, ln_mix_post, ln_ffn_pre,
           w_gate, w_up, w_down, ln_ffn_post):
    bn, seq, d_model = x.shape
    depth = w_in.shape[0]
    bf = jnp.bfloat16
    for l in range(depth):
        x, wg, wu, wd = _mixer(x, ln_mix_pre[l], w_in[l].astype(bf), conv_w[l], pool_w[l].astype(bf),
                               pool_scale[l], w_out[l].astype(bf), ln_mix_post[l],
                               w_gate[l], w_up[l], w_down[l])
        y = _ffn(x.reshape(bn * seq, d_model), ln_ffn_pre[l], wg, wu, wd, ln_ffn_post[l])
        x = y.reshape(bn, seq, d_model)
    return x
```

```python
import functools

import jax
import jax.numpy as jnp
from jax import lax
from jax.experimental import pallas as pl
from jax.experimental.pallas import tpu as pltpu

EPS = 1e-6
CONV_HEADS = 8
CONV_K = 3
POOL_WINDOWS = (2, 4, 8, 16)
HALO = 16

V7X_VMEM_LIMIT_BYTES = 63 * 1024 * 1024 + 512 * 1024
FFN_DOWN_COLS = 512

MIXER_SEQ_TILE = 256
FFN_ROW_TILE = 1024
FFN_DFF_BLOCK = 768


def _rms(xf, axis_size):
    return lax.rsqrt(jnp.sum(xf * xf, axis=-1, keepdims=True) * (1.0 / axis_size) + EPS)


def _mixer_kernel(x_ref, g_pre_ref, w_in_ref, conv_w_ref, pool_w_ref, pool_scale_ref,
                  w_out_ref, g_post_ref, wg_ref, wu_ref, wd_ref,
                  o_ref, wg_o, wu_o, wd_o,
                  cu_hist, v_hist, mixed_buf, *, ts, d_model, cw, pw):
    s = pl.program_id(0)
    n_groups = len(POOL_WINDOWS)
    gd = pw // n_groups
    hd = cw // CONV_HEADS

    @pl.when(s == 0)
    def _():
        cu_hist[...] = jnp.zeros_like(cu_hist)
        v_hist[...] = jnp.zeros_like(v_hist)

    pos = s * ts + lax.broadcasted_iota(jnp.int32, (ts, 1), 0) + 1

    def one_sequence(b):
        x = x_ref[b]
        h = (x * _rms(x, d_model) * g_pre_ref[...]).astype(jnp.bfloat16)

        def proj(lo, width):
            return jnp.dot(h, w_in_ref[:, lo:lo + width], preferred_element_type=jnp.float32)

        v = proj(3 * cw, pw)
        v_ext = jnp.concatenate([v_hist[b], v], axis=0)
        v_hist[b] = v[ts - HALO:, :]

        cu = proj(cw, cw) * proj(2 * cw, cw)
        cu_ext = jnp.concatenate([cu_hist[b], cu], axis=0)
        cu_hist[b] = cu[ts - HALO:, :]
        conv = conv_w_ref[CONV_K - 1:CONV_K, :] * cu
        for k in range(CONV_K - 1):
            delay = CONV_K - 1 - k
            conv = conv + conv_w_ref[k:k + 1, :] * pltpu.roll(cu_ext, delay, axis=0)[HALO:, :]
        y_conv = proj(0, cw) * conv
        for hh in range(CONV_HEADS):
            yh = y_conv[:, hh * hd:(hh + 1) * hd]
            mixed_buf[b, :, hh * hd:(hh + 1) * hd] = (yh * _rms(yh, hd)).astype(jnp.bfloat16)

        for gi, w in enumerate(POOL_WINDOWS):
            lo = gi * gd
            win = v_ext[:, lo:lo + gd]
            k = 1
            while k < w:
                win = win + (pltpu.roll(win, k, axis=0) if k % 8 else
                             jnp.concatenate([win[:k], win[:-k]], axis=0))
                k *= 2
            cnt = jnp.minimum(pos, w).astype(jnp.float32)
            pooled = (win[HALO:, :] / cnt - v[:, lo:lo + gd]).astype(jnp.bfloat16)
            yp = jnp.dot(pooled, pool_w_ref[gi], preferred_element_type=jnp.float32)
            yp = yp * _rms(yp, gd) * pool_scale_ref[:, lo:lo + gd]
            mixed_buf[b, :, cw + lo:cw + lo + gd] = yp.astype(jnp.bfloat16)

        mix_out = jnp.dot(mixed_buf[b], w_out_ref[...], preferred_element_type=jnp.float32)
        o_ref[b] = x + mix_out * _rms(mix_out, d_model) * g_post_ref[...]

    for b in range(x_ref.shape[0]):
        one_sequence(b)

    wg_o[...] = wg_ref[...].astype(jnp.bfloat16)
    wu_o[...] = wu_ref[...].astype(jnp.bfloat16)
    wd_o[...] = wd_ref[...].astype(jnp.bfloat16)


def _mixer(x, g_pre, w_in, conv_w, pool_w, pool_scale, w_out, g_post, w_gate, w_up, w_down):
    bn, seq, d_model = x.shape
    ts = MIXER_SEQ_TILE
    cw = conv_w.shape[-1]
    pw = pool_scale.shape[-1]
    d_ff = w_gate.shape[-1]
    assert seq % ts == 0
    n_steps = seq // ts
    gu_rows = d_model // n_steps
    wd_rows = d_ff // n_steps
    assert gu_rows * n_steps == d_model and gu_rows % 16 == 0
    assert wd_rows * n_steps == d_ff and wd_rows % 16 == 0

    def const(shape):
        return pl.BlockSpec(shape, lambda s: (0,) * len(shape), pipeline_mode=pl.Buffered(1))

    def rows_map(s):
        return (s, 0)

    kern = functools.partial(_mixer_kernel, ts=ts, d_model=d_model, cw=cw, pw=pw)
    bf = jnp.bfloat16
    return pl.pallas_call(
        kern,
        grid=(n_steps,),
        in_specs=[
            pl.BlockSpec((bn, ts, d_model), lambda s: (0, s, 0)),
            const((1, d_model)),
            const(w_in.shape),
            const(conv_w.shape),
            const(pool_w.shape),
            const((1, pw)),
            const(w_out.shape),
            const((1, d_model)),
            pl.BlockSpec((gu_rows, d_ff), rows_map),
            pl.BlockSpec((gu_rows, d_ff), rows_map),
            pl.BlockSpec((wd_rows, d_model), rows_map),
        ],
        out_specs=[
            pl.BlockSpec((bn, ts, d_model), lambda s: (0, s, 0)),
            pl.BlockSpec((gu_rows, d_ff), rows_map),
            pl.BlockSpec((gu_rows, d_ff), rows_map),
            pl.BlockSpec((wd_rows, d_model), rows_map),
        ],
        out_shape=[
            jax.ShapeDtypeStruct(x.shape, x.dtype),
            jax.ShapeDtypeStruct(w_gate.shape, bf),
            jax.ShapeDtypeStruct(w_up.shape, bf),
            jax.ShapeDtypeStruct(w_down.shape, bf),
        ],
        scratch_shapes=[
            pltpu.VMEM((bn, HALO, cw), jnp.float32),
            pltpu.VMEM((bn, HALO, pw), jnp.float32),
            pltpu.VMEM((bn, ts, cw + pw), jnp.bfloat16),
        ],
        compiler_params=pltpu.CompilerParams(
            dimension_semantics=("arbitrary",),
            vmem_limit_bytes=V7X_VMEM_LIMIT_BYTES),
        name="mixer",
    )(x, g_pre.reshape(1, -1), w_in, conv_w, pool_w, pool_scale.reshape(1, -1), w_out,
      g_post.reshape(1, -1), w_gate, w_up, w_down)


def _ffn_kernel(x_ref, g_pre_ref, wg_ref, wu_ref, wd_ref, g_post_ref, o_ref, hf_buf,
                *, d_model, tf, short, short_pos):
    j = pl.program_id(1)
    last = pl.num_programs(1) - 1

    def partial_ffn(hf, width):
        g = jnp.dot(hf, wg_ref[:, :width], preferred_element_type=jnp.float32)
        up = jnp.dot(hf, wu_ref[:, :width], preferred_element_type=jnp.float32)
        act = (g * jax.nn.sigmoid(g) * up).astype(jnp.bfloat16)
        return [(n, jnp.dot(act, wd_ref[:width, n:n + FFN_DOWN_COLS],
                            preferred_element_type=jnp.float32))
                for n in range(0, d_model, FFN_DOWN_COLS)]

    @pl.when(j == 0)
    def _():
        x = x_ref[...]
        hf = (x * _rms(x, d_model) * g_pre_ref[...]).astype(jnp.bfloat16)
        hf_buf[...] = hf
        for n, part in partial_ffn(hf, tf):
            o_ref[:, n:n + FFN_DOWN_COLS] = part

    def accumulate(width):
        for n, part in partial_ffn(hf_buf[...], width):
            o_ref[:, n:n + FFN_DOWN_COLS] += part

    if short_pos is None:
        pl.when(jnp.logical_and(j > 0, j < last))(lambda: accumulate(tf))
    else:
        pl.when(jnp.logical_and(jnp.logical_and(j > 0, j < last), j != short_pos))(
            lambda: accumulate(tf))
        pl.when(j == short_pos)(lambda: accumulate(short))

    @pl.when(j == last)
    def _():
        parts = [part for _, part in partial_ffn(hf_buf[...], tf)]
        ff = o_ref[...] + jnp.concatenate(parts, axis=1)
        o_ref[...] = x_ref[...] + ff * _rms(ff, d_model) * g_post_ref[...]


def _ffn(x, g_pre, w_gate, w_up, w_down, g_post):
    m, d_model = x.shape
    tm, tf = FFN_ROW_TILE, FFN_DFF_BLOCK
    d_ff = w_gate.shape[-1]
    n_blocks = pl.cdiv(d_ff, tf)
    short = d_ff - (n_blocks - 1) * tf
    short_pos = None if short == tf else n_blocks // 2
    assert m % tm == 0 and n_blocks >= 3 and short % 256 == 0

    def block_of(j):
        if short_pos is None:
            return j
        return jnp.where(j < short_pos, j, jnp.where(j == short_pos, n_blocks - 1, j - 1))

    kern = functools.partial(_ffn_kernel, d_model=d_model, tf=tf, short=short, short_pos=short_pos)
    return pl.pallas_call(
        kern,
        grid=(m // tm, n_blocks),
        in_specs=[
            pl.BlockSpec((tm, d_model), lambda i, j: (i, 0)),
            pl.BlockSpec((1, d_model), lambda i, j: (0, 0)),
            pl.BlockSpec((d_model, tf), lambda i, j: (0, block_of(j))),
            pl.BlockSpec((d_model, tf), lambda i, j: (0, block_of(j))),
            pl.BlockSpec((tf, d_model), lambda i, j: (block_of(j), 0)),
            pl.BlockSpec((1, d_model), lambda i, j: (0, 0)),
        ],
        out_specs=pl.BlockSpec((tm, d_model), lambda i, j: (i, 0)),
        out_shape=jax.ShapeDtypeStruct(x.shape, x.dtype),
        scratch_shapes=[pltpu.VMEM((tm, d_model), jnp.bfloat16)],
        compiler_params=pltpu.CompilerParams(
            dimension_semantics=("arbitrary", "arbitrary"),
            vmem_limit_bytes=V7X_VMEM_LIMIT_BYTES),
        name="ffn",
    )(x, g_pre.reshape(1, -1), w_gate, w_up, w_down, g_post.reshape(1, -1))


def kernel(x, ln_mix_pre, w_in, conv_w, pool_w, pool_scale, w_out, ln_mix_post, ln_ffn_pre,
           w_gate, w_up, w_down, ln_ffn_post):
    bn, seq, d_model = x.shape
    depth = w_in.shape[0]
    bf = jnp.bfloat16
    for l in range(depth):
        x, wg, wu, wd = _mixer(x, ln_mix_pre[l], w_in[l].astype(bf), conv_w[l], pool_w[l].astype(bf),
                               pool_scale[l], w_out[l].astype(bf), ln_mix_post[l],
                               w_gate[l], w_up[l], w_down[l])
        y = _ffn(x.reshape(bn * seq, d_model), ln_ffn_pre[l], wg, wu, wd, ln_ffn_post[l])
        x = y.reshape(bn, seq, d_model)
    return x
```

```python
import functools

import jax
import jax.numpy as jnp
from jax import lax
from jax.experimental import pallas as pl
from jax.experimental.pallas import tpu as pltpu

EPS = 1e-6
CONV_HEADS = 8
CONV_K = 3
POOL_WINDOWS = (2, 4, 8, 16)
HALO = 16

V7X_VMEM_LIMIT_BYTES = 63 * 1024 * 1024 + 512 * 1024
FFN_DOWN_COLS = 512
FFN_ROW_GROUPS = 4

MIXER_SEQ_TILE = 256
FFN_ROW_TILE = 1024
FFN_DFF_BLOCK = 1024


def _rms(xf, axis_size):
    return lax.rsqrt(jnp.sum(xf * xf, axis=-1, keepdims=True) * (1.0 / axis_size) + EPS)


def _mixer_kernel(x_ref, g_pre_ref, w_in_ref, conv_w_ref, pool_w_ref, pool_scale_ref,
                  w_out_ref, g_post_ref, wg_ref, wu_ref, wd_ref,
                  o_ref, wg_o, wu_o, wd_o,
                  cu_hist, v_hist, mixed_buf, *, ts, d_model, cw, pw):
    s = pl.program_id(0)
    n_groups = len(POOL_WINDOWS)
    gd = pw // n_groups
    hd = cw // CONV_HEADS

    @pl.when(s == 0)
    def _():
        cu_hist[...] = jnp.zeros_like(cu_hist)
        v_hist[...] = jnp.zeros_like(v_hist)

    pos = s * ts + lax.broadcasted_iota(jnp.int32, (ts, 1), 0) + 1

    def one_sequence(b):
        x = x_ref[b]
        h = (x * _rms(x, d_model) * g_pre_ref[...]).astype(jnp.bfloat16)

        def proj(lo, width):
            return jnp.dot(h, w_in_ref[:, lo:lo + width], preferred_element_type=jnp.float32)

        v = proj(3 * cw, pw)
        v_ext = jnp.concatenate([v_hist[b], v], axis=0)
        v_hist[b] = v[ts - HALO:, :]

        cu = proj(cw, cw) * proj(2 * cw, cw)
        cu_ext = jnp.concatenate([cu_hist[b], cu], axis=0)
        cu_hist[b] = cu[ts - HALO:, :]
        conv = conv_w_ref[CONV_K - 1:CONV_K, :] * cu
        for k in range(CONV_K - 1):
            delay = CONV_K - 1 - k
            conv = conv + conv_w_ref[k:k + 1, :] * pltpu.roll(cu_ext, delay, axis=0)[HALO:, :]
        y_conv = proj(0, cw) * conv
        for hh in range(CONV_HEADS):
            yh = y_conv[:, hh * hd:(hh + 1) * hd]
            mixed_buf[b, :, hh * hd:(hh + 1) * hd] = (yh * _rms(yh, hd)).astype(jnp.bfloat16)

        for gi, w in enumerate(POOL_WINDOWS):
            lo = gi * gd
            win = v_ext[:, lo:lo + gd]
            k = 1
            while k < w:
                win = win + (pltpu.roll(win, k, axis=0) if k % 8 else
                             jnp.concatenate([win[:k], win[:-k]], axis=0))
                k *= 2
            cnt = jnp.minimum(pos, w).astype(jnp.float32)
            pooled = (win[HALO:, :] / cnt - v[:, lo:lo + gd]).astype(jnp.bfloat16)
            yp = jnp.dot(pooled, pool_w_ref[gi], preferred_element_type=jnp.float32)
            yp = yp * _rms(yp, gd) * pool_scale_ref[:, lo:lo + gd]
            mixed_buf[b, :, cw + lo:cw + lo + gd] = yp.astype(jnp.bfloat16)

        mix_out = jnp.dot(mixed_buf[b], w_out_ref[...], preferred_element_type=jnp.float32)
        o_ref[b] = x + mix_out * _rms(mix_out, d_model) * g_post_ref[...]

    for b in range(x_ref.shape[0]):
        one_sequence(b)

    wg_o[...] = wg_ref[...].astype(jnp.bfloat16)
    wu_o[...] = wu_ref[...].astype(jnp.bfloat16)
    wd_o[...] = wd_ref[...].astype(jnp.bfloat16)


def _mixer(x, g_pre, w_in, conv_w, pool_w, pool_scale, w_out, g_post, w_gate, w_up, w_down):
    bn, seq, d_model = x.shape
    ts = MIXER_SEQ_TILE
    cw = conv_w.shape[-1]
    pw = pool_scale.shape[-1]
    d_ff = w_gate.shape[-1]
    assert seq % ts == 0
    n_steps = seq // ts
    gu_rows = d_model // n_steps
    wd_rows = d_ff // n_steps
    assert gu_rows * n_steps == d_model and gu_rows % 16 == 0
    assert wd_rows * n_steps == d_ff and wd_rows % 16 == 0

    def const(shape):
        return pl.BlockSpec(shape, lambda s: (0,) * len(shape), pipeline_mode=pl.Buffered(1))

    def rows_map(s):
        return (s, 0)

    kern = functools.partial(_mixer_kernel, ts=ts, d_model=d_model, cw=cw, pw=pw)
    bf = jnp.bfloat16
    return pl.pallas_call(
        kern,
        grid=(n_steps,),
        in_specs=[
            pl.BlockSpec((bn, ts, d_model), lambda s: (0, s, 0)),
            const((1, d_model)),
            const(w_in.shape),
            const(conv_w.shape),
            const(pool_w.shape),
            const((1, pw)),
            const(w_out.shape),
            const((1, d_model)),
            pl.BlockSpec((gu_rows, d_ff), rows_map),
            pl.BlockSpec((gu_rows, d_ff), rows_map),
            pl.BlockSpec((wd_rows, d_model), rows_map),
        ],
        out_specs=[
            pl.BlockSpec((bn, ts, d_model), lambda s: (0, s, 0)),
            pl.BlockSpec((gu_rows, d_ff), rows_map),
            pl.BlockSpec((gu_rows, d_ff), rows_map),
            pl.BlockSpec((wd_rows, d_model), rows_map),
        ],
        out_shape=[
            jax.ShapeDtypeStruct(x.shape, x.dtype),
            jax.ShapeDtypeStruct(w_gate.shape, bf),
            jax.ShapeDtypeStruct(w_up.shape, bf),
            jax.ShapeDtypeStruct(w_down.shape, bf),
        ],
        scratch_shapes=[
            pltpu.VMEM((bn, HALO, cw), jnp.float32),
            pltpu.VMEM((bn, HALO, pw), jnp.float32),
            pltpu.VMEM((bn, ts, cw + pw), jnp.bfloat16),
        ],
        compiler_params=pltpu.CompilerParams(
            dimension_semantics=("arbitrary",),
            vmem_limit_bytes=V7X_VMEM_LIMIT_BYTES),
        name="mixer",
    )(x, g_pre.reshape(1, -1), w_in, conv_w, pool_w, pool_scale.reshape(1, -1), w_out,
      g_post.reshape(1, -1), w_gate, w_up, w_down)


def _ffn_kernel(x_ref, g_pre_ref, wg_ref, wu_ref, wd_ref, g_post_ref, o_ref, hf_buf,
                *, d_model, tf, short, short_pos):
    j = pl.program_id(1)
    last = pl.num_programs(1) - 1

    def partial_ffn(hf, width):
        g = jnp.dot(hf, wg_ref[:, :width], preferred_element_type=jnp.float32)
        up = jnp.dot(hf, wu_ref[:, :width], preferred_element_type=jnp.float32)
        act = (g * jax.nn.sigmoid(g) * up).astype(jnp.bfloat16)
        return [(n, jnp.dot(act, wd_ref[:width, n:n + FFN_DOWN_COLS],
                            preferred_element_type=jnp.float32))
                for n in range(0, d_model, FFN_DOWN_COLS)]

    tm = x_ref.shape[0]
    groups = [slice(r, r + tm // FFN_ROW_GROUPS) for r in range(0, tm, tm // FFN_ROW_GROUPS)]

    @pl.when(j == 0)
    def _():
        for rows in groups:
            x = x_ref[rows, :]
            hf = (x * _rms(x, d_model) * g_pre_ref[...]).astype(jnp.bfloat16)
            hf_buf[rows, :] = hf
            for n, part in partial_ffn(hf, tf):
                o_ref[rows, n:n + FFN_DOWN_COLS] = part

    def accumulate(width):
        for rows in groups:
            for n, part in partial_ffn(hf_buf[rows, :], width):
                o_ref[rows, n:n + FFN_DOWN_COLS] += part

    if short_pos is None:
        pl.when(jnp.logical_and(j > 0, j < last))(lambda: accumulate(tf))
    else:
        pl.when(jnp.logical_and(jnp.logical_and(j > 0, j < last), j != short_pos))(
            lambda: accumulate(tf))
        pl.when(j == short_pos)(lambda: accumulate(short))

    @pl.when(j == last)
    def _():
        for rows in groups:
            parts = [part for _, part in partial_ffn(hf_buf[rows, :], tf)]
            ff = o_ref[rows, :] + jnp.concatenate(parts, axis=1)
            o_ref[rows, :] = x_ref[rows, :] + ff * _rms(ff, d_model) * g_post_ref[...]


def _ffn(x, g_pre, w_gate, w_up, w_down, g_post):
    m, d_model = x.shape
    tm, tf = FFN_ROW_TILE, FFN_DFF_BLOCK
    d_ff = w_gate.shape[-1]
    n_blocks = pl.cdiv(d_ff, tf)
    short = d_ff - (n_blocks - 1) * tf
    short_pos = None if short == tf else n_blocks // 2
    assert m % tm == 0 and n_blocks >= 3 and short % 256 == 0

    def block_of(j):
        if short_pos is None:
            return j
        return jnp.where(j < short_pos, j, jnp.where(j == short_pos, n_blocks - 1, j - 1))

    kern = functools.partial(_ffn_kernel, d_model=d_model, tf=tf, short=short, short_pos=short_pos)
    return pl.pallas_call(
        kern,
        grid=(m // tm, n_blocks),
        in_specs=[
            pl.BlockSpec((tm, d_model), lambda i, j: (i, 0)),
            pl.BlockSpec((1, d_model), lambda i, j: (0, 0)),
            pl.BlockSpec((d_model, tf), lambda i, j: (0, block_of(j))),
            pl.BlockSpec((d_model, tf), lambda i, j: (0, block_of(j))),
            pl.BlockSpec((tf, d_model), lambda i, j: (block_of(j), 0)),
            pl.BlockSpec((1, d_model), lambda i, j: (0, 0)),
        ],
        out_specs=pl.BlockSpec((tm, d_model), lambda i, j: (i, 0)),
        out_shape=jax.ShapeDtypeStruct(x.shape, x.dtype),
        scratch_shapes=[pltpu.VMEM((tm, d_model), jnp.bfloat16)],
        compiler_params=pltpu.CompilerParams(
            dimension_semantics=("arbitrary", "arbitrary"),
            vmem_limit_bytes=V7X_VMEM_LIMIT_BYTES),
        name="ffn",
    )(x, g_pre.reshape(1, -1), w_gate, w_up, w_down, g_post.reshape(1, -1))


def kernel(x, ln_mix_pre, w_in, conv_w, pool_w, pool_scale, w_out, ln_mix_post, ln_ffn_pre,
           w_gate, w_up, w_down, ln_ffn_post):
    bn, seq, d_model = x.shape
    depth = w_in.shape[0]
    bf = jnp.bfloat16
    for l in range(depth):
        x, wg, wu, wd = _mixer(x, ln_mix_pre[l], w_in[l].astype(bf), conv_w[l], pool_w[l].astype(bf),
                               pool_scale[l], w_out[l].astype(bf), ln_mix_post[l],
                               w_gate[l], w_up[l], w_down[l])
        y = _ffn(x.reshape(bn * seq, d_model), ln_ffn_pre[l], wg, wu, wd, ln_ffn_post[l])
        x = y.reshape(bn, seq, d_model)
    return x
```

```python
import functools

import jax
import jax.numpy as jnp
from jax import lax
from jax.experimental import pallas as pl
from jax.experimental.pallas import tpu as pltpu

EPS = 1e-6
CONV_HEADS = 8
CONV_K = 3
POOL_WINDOWS = (2, 4, 8, 16)
HALO = 16

V7X_VMEM_LIMIT_BYTES = 62 * 1024 * 1024
FFN_DOWN_COLS = 512
FFN_EDGE_GROUPS = 2

MIXER_SEQ_TILE = 256
FFN_ROW_TILE = 1024
FFN_DFF_BLOCK = 768


def _rms(xf, axis_size):
    return lax.rsqrt(jnp.sum(xf * xf, axis=-1, keepdims=True) * (1.0 / axis_size) + EPS)


def _mixer_kernel(x_ref, g_pre_ref, w_in_ref, conv_w_ref, pool_w_ref, pool_scale_ref,
                  w_out_ref, g_post_ref, wg_ref, wu_ref, wd_ref,
                  o_ref, wg_o, wu_o, wd_o,
                  cu_hist, v_hist, mixed_buf, *, ts, d_model, cw, pw):
    s = pl.program_id(0)
    n_groups = len(POOL_WINDOWS)
    gd = pw // n_groups
    hd = cw // CONV_HEADS

    @pl.when(s == 0)
    def _():
        cu_hist[...] = jnp.zeros_like(cu_hist)
        v_hist[...] = jnp.zeros_like(v_hist)

    pos = s * ts + lax.broadcasted_iota(jnp.int32, (ts, 1), 0) + 1

    def one_sequence(b):
        x = x_ref[b]
        h = (x * _rms(x, d_model) * g_pre_ref[...]).astype(jnp.bfloat16)

        def proj(lo, width):
            return jnp.dot(h, w_in_ref[:, lo:lo + width], preferred_element_type=jnp.float32)

        v = proj(3 * cw, pw)
        v_ext = jnp.concatenate([v_hist[b], v], axis=0)
        v_hist[b] = v[ts - HALO:, :]

        cu = proj(cw, cw) * proj(2 * cw, cw)
        cu_ext = jnp.concatenate([cu_hist[b], cu], axis=0)
        cu_hist[b] = cu[ts - HALO:, :]
        conv = conv_w_ref[CONV_K - 1:CONV_K, :] * cu
        for k in range(CONV_K - 1):
            delay = CONV_K - 1 - k
            conv = conv + conv_w_ref[k:k + 1, :] * pltpu.roll(cu_ext, delay, axis=0)[HALO:, :]
        y_conv = proj(0, cw) * conv
        for hh in range(CONV_HEADS):
            yh = y_conv[:, hh * hd:(hh + 1) * hd]
            mixed_buf[b, :, hh * hd:(hh + 1) * hd] = (yh * _rms(yh, hd)).astype(jnp.bfloat16)

        for gi, w in enumerate(POOL_WINDOWS):
            lo = gi * gd
            win = v_ext[:, lo:lo + gd]
            k = 1
            while k < w:
                win = win + (pltpu.roll(win, k, axis=0) if k % 8 else
                             jnp.concatenate([win[:k], win[:-k]], axis=0))
                k *= 2
            cnt = jnp.minimum(pos, w).astype(jnp.float32)
            pooled = (win[HALO:, :] / cnt - v[:, lo:lo + gd]).astype(jnp.bfloat16)
            yp = jnp.dot(pooled, pool_w_ref[gi], preferred_element_type=jnp.float32)
            yp = yp * _rms(yp, gd) * pool_scale_ref[:, lo:lo + gd]
            mixed_buf[b, :, cw + lo:cw + lo + gd] = yp.astype(jnp.bfloat16)

        mix_out = jnp.dot(mixed_buf[b], w_out_ref[...], preferred_element_type=jnp.float32)
        o_ref[b] = x + mix_out * _rms(mix_out, d_model) * g_post_ref[...]

    for b in range(x_ref.shape[0]):
        one_sequence(b)

    wg_o[...] = wg_ref[...].astype(jnp.bfloat16)
    wu_o[...] = wu_ref[...].astype(jnp.bfloat16)
    wd_o[...] = wd_ref[...].astype(jnp.bfloat16)


def _mixer(x, g_pre, w_in, conv_w, pool_w, pool_scale, w_out, g_post, w_gate, w_up, w_down):
    bn, seq, d_model = x.shape
    ts = MIXER_SEQ_TILE
    cw = conv_w.shape[-1]
    pw = pool_scale.shape[-1]
    d_ff = w_gate.shape[-1]
    assert seq % ts == 0
    n_steps = seq // ts
    gu_rows = d_model // n_steps
    wd_rows = d_ff // n_steps
    assert gu_rows * n_steps == d_model and gu_rows % 16 == 0
    assert wd_rows * n_steps == d_ff and wd_rows % 16 == 0

    def const(shape):
        return pl.BlockSpec(shape, lambda s: (0,) * len(shape), pipeline_mode=pl.Buffered(1))

    def rows_map(s):
        return (s, 0)

    kern = functools.partial(_mixer_kernel, ts=ts, d_model=d_model, cw=cw, pw=pw)
    bf = jnp.bfloat16
    return pl.pallas_call(
        kern,
        grid=(n_steps,),
        in_specs=[
            pl.BlockSpec((bn, ts, d_model), lambda s: (0, s, 0)),
            const((1, d_model)),
            const(w_in.shape),
            const(conv_w.shape),
            const(pool_w.shape),
            const((1, pw)),
            const(w_out.shape),
            const((1, d_model)),
            pl.BlockSpec((gu_rows, d_ff), rows_map),
            pl.BlockSpec((gu_rows, d_ff), rows_map),
            pl.BlockSpec((wd_rows, d_model), rows_map),
        ],
        out_specs=[
            pl.BlockSpec((bn, ts, d_model), lambda s: (0, s, 0)),
            pl.BlockSpec((gu_rows, d_ff), rows_map),
            pl.BlockSpec((gu_rows, d_ff), rows_map),
            pl.BlockSpec((wd_rows, d_model), rows_map),
        ],
        out_shape=[
            jax.ShapeDtypeStruct(x.shape, x.dtype),
            jax.ShapeDtypeStruct(w_gate.shape, bf),
            jax.ShapeDtypeStruct(w_up.shape, bf),
            jax.ShapeDtypeStruct(w_down.shape, bf),
        ],
        scratch_shapes=[
            pltpu.VMEM((bn, HALO, cw), jnp.float32),
            pltpu.VMEM((bn, HALO, pw), jnp.float32),
            pltpu.VMEM((bn, ts, cw + pw), jnp.bfloat16),
        ],
        compiler_params=pltpu.CompilerParams(
            dimension_semantics=("arbitrary",),
            vmem_limit_bytes=V7X_VMEM_LIMIT_BYTES),
        name="mixer",
    )(x, g_pre.reshape(1, -1), w_in, conv_w, pool_w, pool_scale.reshape(1, -1), w_out,
      g_post.reshape(1, -1), w_gate, w_up, w_down)


def _ffn_kernel(x_ref, g_pre_ref, wg_ref, wu_ref, wd_ref, g_post_ref, o_ref, hf_buf,
                *, d_model, tf, short, short_pos):
    j = pl.program_id(1)
    last = pl.num_programs(1) - 1

    def partial_ffn(hf, width):
        g = jnp.dot(hf, wg_ref[:, :width], preferred_element_type=jnp.float32)
        up = jnp.dot(hf, wu_ref[:, :width], preferred_element_type=jnp.float32)
        act = (g * jax.nn.sigmoid(g) * up).astype(jnp.bfloat16)
        return [(n, jnp.dot(act, wd_ref[:width, n:n + FFN_DOWN_COLS],
                            preferred_element_type=jnp.float32))
                for n in range(0, d_model, FFN_DOWN_COLS)]

    tm = x_ref.shape[0]
    groups = [slice(r, r + tm // FFN_EDGE_GROUPS) for r in range(0, tm, tm // FFN_EDGE_GROUPS)]

    @pl.when(j == 0)
    def _():
        for rows in groups:
            x = x_ref[rows, :]
            hf = (x * _rms(x, d_model) * g_pre_ref[...]).astype(jnp.bfloat16)
            hf_buf[rows, :] = hf
            for n, part in partial_ffn(hf, tf):
                o_ref[rows, n:n + FFN_DOWN_COLS] = part

    def accumulate(width):
        for n, part in partial_ffn(hf_buf[...], width):
            o_ref[:, n:n + FFN_DOWN_COLS] += part

    if short_pos is None:
        pl.when(jnp.logical_and(j > 0, j < last))(lambda: accumulate(tf))
    else:
        pl.when(jnp.logical_and(jnp.logical_and(j > 0, j < last), j != short_pos))(
            lambda: accumulate(tf))
        pl.when(j == short_pos)(lambda: accumulate(short))

    @pl.when(j == last)
    def _():
        for rows in groups:
            parts = [part for _, part in partial_ffn(hf_buf[rows, :], tf)]
            ff = o_ref[rows, :] + jnp.concatenate(parts, axis=1)
            o_ref[rows, :] = x_ref[rows, :] + ff * _rms(ff, d_model) * g_post_ref[...]


def _ffn(x, g_pre, w_gate, w_up, w_down, g_post):
    m, d_model = x.shape
    tm, tf = FFN_ROW_TILE, FFN_DFF_BLOCK
    d_ff = w_gate.shape[-1]
    n_blocks = pl.cdiv(d_ff, tf)
    short = d_ff - (n_blocks - 1) * tf
    short_pos = None if short == tf else n_blocks // 2
    assert m % tm == 0 and n_blocks >= 3 and short % 256 == 0

    def block_of(j):
        if short_pos is None:
            return j
        return jnp.where(j < short_pos, j, jnp.where(j == short_pos, n_blocks - 1, j - 1))

    kern = functools.partial(_ffn_kernel, d_model=d_model, tf=tf, short=short, short_pos=short_pos)
    return pl.pallas_call(
        kern,
        grid=(m // tm, n_blocks),
        in_specs=[
            pl.BlockSpec((tm, d_model), lambda i, j: (i, 0)),
            pl.BlockSpec((1, d_model), lambda i, j: (0, 0)),
            pl.BlockSpec((d_model, tf), lambda i, j: (0, block_of(j))),
            pl.BlockSpec((d_model, tf), lambda i, j: (0, block_of(j))),
            pl.BlockSpec((tf, d_model), lambda i, j: (block_of(j), 0)),
            pl.BlockSpec((1, d_model), lambda i, j: (0, 0)),
        ],
        out_specs=pl.BlockSpec((tm, d_model), lambda i, j: (i, 0)),
        out_shape=jax.ShapeDtypeStruct(x.shape, x.dtype),
        scratch_shapes=[pltpu.VMEM((tm, d_model), jnp.bfloat16)],
        compiler_params=pltpu.CompilerParams(
            dimension_semantics=("arbitrary", "arbitrary"),
            vmem_limit_bytes=V7X_VMEM_LIMIT_BYTES),
        name="ffn",
    )(x, g_pre.reshape(1, -1), w_gate, w_up, w_down, g_post.reshape(1, -1))


def kernel(x, ln_mix_pre, w_in, conv_w, pool_w, pool_scale, w_out, ln_mix_post, ln_ffn_pre,
           w_gate, w_up, w_down, ln_ffn_post):
    bn, seq, d_model = x.shape
    depth = w_in.shape[0]
    bf = jnp.bfloat16
    for l in range(depth):
        x, wg, wu, wd = _mixer(x, ln_mix_pre[l], w_in[l].astype(bf), conv_w[l], pool_w[l].astype(bf),
                               pool_scale[l], w_out[l].astype(bf), ln_mix_post[l],
                               w_gate[l], w_up[l], w_down[l])
        y = _ffn(x.reshape(bn * seq, d_model), ln_ffn_pre[l], wg, wu, wd, ln_ffn_post[l])
        x = y.reshape(bn, seq, d_model)
    return x
```

```python
import functools

import jax
import jax.numpy as jnp
from jax import lax
from jax.experimental import pallas as pl
from jax.experimental.pallas import tpu as pltpu

EPS = 1e-6
CONV_HEADS = 8
CONV_K = 3
POOL_WINDOWS = (2, 4, 8, 16)
HALO = 16

V7X_VMEM_LIMIT_BYTES = 62 * 1024 * 1024
FFN_DOWN_COLS = 512
FFN_EDGE_GROUPS = 2

MIXER_SEQ_TILE = 256
FFN_ROW_TILE = 1024
FFN_DFF_BLOCK = 768


def _rms(xf, axis_size):
    return lax.rsqrt(jnp.sum(xf * xf, axis=-1, keepdims=True) * (1.0 / axis_size) + EPS)


def _mixer_kernel(x_ref, g_pre_ref, w_in_ref, conv_w_ref, pool_w_ref, pool_scale_ref,
                  w_out_ref, g_post_ref, wg_ref, wu_ref, wd_ref,
                  o_ref, wg_o, wu_o, wd_o,
                  cu_hist, v_hist, mixed_buf, *, ts, d_model, cw, pw):
    s = pl.program_id(0)
    n_groups = len(POOL_WINDOWS)
    gd = pw // n_groups
    hd = cw // CONV_HEADS

    @pl.when(s == 0)
    def _():
        cu_hist[...] = jnp.zeros_like(cu_hist)
        v_hist[...] = jnp.zeros_like(v_hist)

    pos = s * ts + lax.broadcasted_iota(jnp.int32, (ts, 1), 0) + 1

    n_rows = x_ref.shape[0]
    xs = [x_ref[b] for b in range(n_rows)]
    h = jnp.concatenate(
        [(x * _rms(x, d_model) * g_pre_ref[...]).astype(jnp.bfloat16) for x in xs], axis=0)

    def proj(lo, width):
        return jnp.dot(h, w_in_ref[:, lo:lo + width], preferred_element_type=jnp.float32)

    v_all = proj(3 * cw, pw)
    wg_o[...] = wg_ref[...].astype(jnp.bfloat16)
    wu_o[...] = wu_ref[...].astype(jnp.bfloat16)
    wd_o[...] = wd_ref[...].astype(jnp.bfloat16)
    cu_all = proj(cw, cw) * proj(2 * cw, cw)
    b_all = proj(0, cw)

    for b in range(n_rows):
        rows = slice(b * ts, (b + 1) * ts)
        v = v_all[rows]
        v_ext = jnp.concatenate([v_hist[b], v], axis=0)
        v_hist[b] = v[ts - HALO:, :]

        cu = cu_all[rows]
        cu_ext = jnp.concatenate([cu_hist[b], cu], axis=0)
        cu_hist[b] = cu[ts - HALO:, :]
        conv = conv_w_ref[CONV_K - 1:CONV_K, :] * cu
        for k in range(CONV_K - 1):
            delay = CONV_K - 1 - k
            conv = conv + conv_w_ref[k:k + 1, :] * pltpu.roll(cu_ext, delay, axis=0)[HALO:, :]
        y_conv = b_all[rows] * conv
        for hh in range(CONV_HEADS):
            yh = y_conv[:, hh * hd:(hh + 1) * hd]
            mixed_buf[b, :, hh * hd:(hh + 1) * hd] = (yh * _rms(yh, hd)).astype(jnp.bfloat16)

        for gi, w in enumerate(POOL_WINDOWS):
            lo = gi * gd
            win = v_ext[:, lo:lo + gd]
            k = 1
            while k < w:
                win = win + (pltpu.roll(win, k, axis=0) if k % 8 else
                             jnp.concatenate([win[:k], win[:-k]], axis=0))
                k *= 2
            cnt = jnp.minimum(pos, w).astype(jnp.float32)
            pooled = (win[HALO:, :] / cnt - v[:, lo:lo + gd]).astype(jnp.bfloat16)
            yp = jnp.dot(pooled, pool_w_ref[gi], preferred_element_type=jnp.float32)
            yp = yp * _rms(yp, gd) * pool_scale_ref[:, lo:lo + gd]
            mixed_buf[b, :, cw + lo:cw + lo + gd] = yp.astype(jnp.bfloat16)

    for b in range(n_rows):
        mix_out = jnp.dot(mixed_buf[b], w_out_ref[...], preferred_element_type=jnp.float32)
        o_ref[b] = xs[b] + mix_out * _rms(mix_out, d_model) * g_post_ref[...]


def _mixer(x, g_pre, w_in, conv_w, pool_w, pool_scale, w_out, g_post, w_gate, w_up, w_down):
    bn, seq, d_model = x.shape
    ts = MIXER_SEQ_TILE
    cw = conv_w.shape[-1]
    pw = pool_scale.shape[-1]
    d_ff = w_gate.shape[-1]
    assert seq % ts == 0
    n_steps = seq // ts
    gu_rows = d_model // n_steps
    wd_rows = d_ff // n_steps
    assert gu_rows * n_steps == d_model and gu_rows % 16 == 0
    assert wd_rows * n_steps == d_ff and wd_rows % 16 == 0

    def const(shape):
        return pl.BlockSpec(shape, lambda s: (0,) * len(shape), pipeline_mode=pl.Buffered(1))

    def rows_map(s):
        return (s, 0)

    kern = functools.partial(_mixer_kernel, ts=ts, d_model=d_model, cw=cw, pw=pw)
    bf = jnp.bfloat16
    return pl.pallas_call(
        kern,
        grid=(n_steps,),
        in_specs=[
            pl.BlockSpec((bn, ts, d_model), lambda s: (0, s, 0)),
            const((1, d_model)),
            const(w_in.shape),
            const(conv_w.shape),
            const(pool_w.shape),
            const((1, pw)),
            const(w_out.shape),
            const((1, d_model)),
            pl.BlockSpec((gu_rows, d_ff), rows_map),
            pl.BlockSpec((gu_rows, d_ff), rows_map),
            pl.BlockSpec((wd_rows, d_model), rows_map),
        ],
        out_specs=[
            pl.BlockSpec((bn, ts, d_model), lambda s: (0, s, 0)),
            pl.BlockSpec((gu_rows, d_ff), rows_map),
            pl.BlockSpec((gu_rows, d_ff), rows_map),
            pl.BlockSpec((wd_rows, d_model), rows_map),
        ],
        out_shape=[
            jax.ShapeDtypeStruct(x.shape, x.dtype),
            jax.ShapeDtypeStruct(w_gate.shape, bf),
            jax.ShapeDtypeStruct(w_up.shape, bf),
            jax.ShapeDtypeStruct(w_down.shape, bf),
        ],
        scratch_shapes=[
            pltpu.VMEM((bn, HALO, cw), jnp.float32),
            pltpu.VMEM((bn, HALO, pw), jnp.float32),
            pltpu.VMEM((bn, ts, cw + pw), jnp.bfloat16),
        ],
        compiler_params=pltpu.CompilerParams(
            dimension_semantics=("arbitrary",),
            vmem_limit_bytes=V7X_VMEM_LIMIT_BYTES),
        name="mixer",
    )(x, g_pre.reshape(1, -1), w_in, conv_w, pool_w, pool_scale.reshape(1, -1), w_out,
      g_post.reshape(1, -1), w_gate, w_up, w_down)


def _ffn_kernel(x_ref, g_pre_ref, wg_ref, wu_ref, wd_ref, g_post_ref, o_ref, hf_buf,
                *, d_model, tf, short, short_pos):
    j = pl.program_id(1)
    last = pl.num_programs(1) - 1

    def partial_ffn(hf, width):
        g = jnp.dot(hf, wg_ref[:, :width], preferred_element_type=jnp.float32)
        up = jnp.dot(hf, wu_ref[:, :width], preferred_element_type=jnp.float32)
        act = (g * jax.nn.sigmoid(g) * up).astype(jnp.bfloat16)
        return [(n, jnp.dot(act, wd_ref[:width, n:n + FFN_DOWN_COLS],
                            preferred_element_type=jnp.float32))
                for n in range(0, d_model, FFN_DOWN_COLS)]

    tm = x_ref.shape[0]
    groups = [slice(r, r + tm // FFN_EDGE_GROUPS) for r in range(0, tm, tm // FFN_EDGE_GROUPS)]

    @pl.when(j == 0)
    def _():
        for rows in groups:
            x = x_ref[rows, :]
            hf = (x * _rms(x, d_model) * g_pre_ref[...]).astype(jnp.bfloat16)
            hf_buf[rows, :] = hf
            for n, part in partial_ffn(hf, tf):
                o_ref[rows, n:n + FFN_DOWN_COLS] = part

    def accumulate(width):
        for n, part in partial_ffn(hf_buf[...], width):
            o_ref[:, n:n + FFN_DOWN_COLS] += part

    if short_pos is None:
        pl.when(jnp.logical_and(j > 0, j < last))(lambda: accumulate(tf))
    else:
        pl.when(jnp.logical_and(jnp.logical_and(j > 0, j < last), j != short_pos))(
            lambda: accumulate(tf))
        pl.when(j == short_pos)(lambda: accumulate(short))

    @pl.when(j == last)
    def _():
        for rows in groups:
            parts = [part for _, part in partial_ffn(hf_buf[rows, :], tf)]
            ff = o_ref[rows, :] + jnp.concatenate(parts, axis=1)
            o_ref[rows, :] = x_ref[rows, :] + ff * _rms(ff, d_model) * g_post_ref[...]


def _ffn(x, g_pre, w_gate, w_up, w_down, g_post):
    m, d_model = x.shape
    tm, tf = FFN_ROW_TILE, FFN_DFF_BLOCK
    d_ff = w_gate.shape[-1]
    n_blocks = pl.cdiv(d_ff, tf)
    short = d_ff - (n_blocks - 1) * tf
    short_pos = None if short == tf else n_blocks // 2
    assert m % tm == 0 and n_blocks >= 3 and short % 256 == 0

    def block_of(j):
        if short_pos is None:
            return j
        return jnp.where(j < short_pos, j, jnp.where(j == short_pos, n_blocks - 1, j - 1))

    kern = functools.partial(_ffn_kernel, d_model=d_model, tf=tf, short=short, short_pos=short_pos)
    return pl.pallas_call(
        kern,
        grid=(m // tm, n_blocks),
        in_specs=[
            pl.BlockSpec((tm, d_model), lambda i, j: (i, 0)),
            pl.BlockSpec((1, d_model), lambda i, j: (0, 0)),
            pl.BlockSpec((d_model, tf), lambda i, j: (0, block_of(j))),
            pl.BlockSpec((d_model, tf), lambda i, j: (0, block_of(j))),
            pl.BlockSpec((tf, d_model), lambda i, j: (block_of(j), 0)),
            pl.BlockSpec((1, d_model), lambda i, j: (0, 0)),
        ],
        out_specs=pl.BlockSpec((tm, d_model), lambda i, j: (i, 0)),
        out_shape=jax.ShapeDtypeStruct(x.shape, x.dtype),
        scratch_shapes=[pltpu.VMEM((tm, d_model), jnp.bfloat16)],
        compiler_params=pltpu.CompilerParams(
            dimension_semantics=("arbitrary", "arbitrary"),
            vmem_limit_bytes=V7X_VMEM_LIMIT_BYTES),
        name="ffn",
    )(x, g_pre.reshape(1, -1), w_gate, w_up, w_down, g_post.reshape(1, -1))


def kernel(x, ln_mix_pre, w_in, conv_w, pool_w, pool_scale, w_out, ln_mix_post, ln_ffn_pre,
           w_gate, w_up, w_down, ln_ffn_post):
    bn, seq, d_model = x.shape
    depth = w_in.shape[0]
    bf = jnp.bfloat16
    for l in range(depth):
        x, wg, wu, wd = _mixer(x, ln_mix_pre[l], w_in[l].astype(bf), conv_w[l], pool_w[l].astype(bf),
                               pool_scale[l], w_out[l].astype(bf), ln_mix_post[l],
                               w_gate[l], w_up[l], w_down[l])
        y = _ffn(x.reshape(bn * seq, d_model), ln_ffn_pre[l], wg, wu, wd, ln_ffn_post[l])
        x = y.reshape(bn, seq, d_model)
    return x
```

```python
import functools

import jax
import jax.numpy as jnp
from jax import lax
from jax.experimental import pallas as pl
from jax.experimental.pallas import tpu as pltpu

EPS = 1e-6
CONV_HEADS = 8
CONV_K = 3
POOL_WINDOWS = (2, 4, 8, 16)
HALO = 16

V7X_VMEM_LIMIT_BYTES = 62 * 1024 * 1024
FFN_DOWN_COLS = 512
FFN_EDGE_GROUPS = 2

MIXER_SEQ_TILE = 256
FFN_ROW_TILE = 1024
FFN_DFF_BLOCK = 768


def _rms(xf, axis_size):
    return lax.rsqrt(jnp.sum(xf * xf, axis=-1, keepdims=True) * (1.0 / axis_size) + EPS)


def _mixer_kernel(x_ref, g_pre_ref, w_in_ref, conv_w_ref, pool_w_ref, pool_scale_ref,
                  w_out_ref, g_post_ref, wg_ref, wu_ref, wd_ref,
                  o_ref, wg_o, wu_o, wd_o,
                  cu_hist, v_hist, mixed_buf, *, ts, d_model, cw, pw):
    s = pl.program_id(0)
    n_groups = len(POOL_WINDOWS)
    gd = pw // n_groups
    hd = cw // CONV_HEADS

    @pl.when(s == 0)
    def _():
        cu_hist[...] = jnp.zeros_like(cu_hist)
        v_hist[...] = jnp.zeros_like(v_hist)

    pos = s * ts + lax.broadcasted_iota(jnp.int32, (ts, 1), 0) + 1

    n_rows = x_ref.shape[0]
    xs = [x_ref[b] for b in range(n_rows)]
    h = jnp.concatenate(
        [(x * _rms(x, d_model) * g_pre_ref[...]).astype(jnp.bfloat16) for x in xs], axis=0)

    def proj(lo, width):
        return jnp.dot(h, w_in_ref[:, lo:lo + width], preferred_element_type=jnp.float32)

    v_all = proj(3 * cw, pw)
    wg_o[...] = wg_ref[...].astype(jnp.bfloat16)
    wu_o[...] = wu_ref[...].astype(jnp.bfloat16)
    wd_o[...] = wd_ref[...].astype(jnp.bfloat16)
    cu_all = proj(cw, cw) * proj(2 * cw, cw)
    b_all = proj(0, cw)

    pool_w = [pool_w_ref[gi].astype(jnp.bfloat16) for gi in range(n_groups)]

    for b in range(n_rows):
        rows = slice(b * ts, (b + 1) * ts)
        v = v_all[rows]
        v_ext = jnp.concatenate([v_hist[b], v], axis=0)
        v_hist[b] = v[ts - HALO:, :]

        cu = cu_all[rows]
        cu_ext = jnp.concatenate([cu_hist[b], cu], axis=0)
        cu_hist[b] = cu[ts - HALO:, :]
        conv = conv_w_ref[CONV_K - 1:CONV_K, :] * cu
        for k in range(CONV_K - 1):
            delay = CONV_K - 1 - k
            conv = conv + conv_w_ref[k:k + 1, :] * pltpu.roll(cu_ext, delay, axis=0)[HALO:, :]
        y_conv = b_all[rows] * conv
        for hh in range(CONV_HEADS):
            yh = y_conv[:, hh * hd:(hh + 1) * hd]
            mixed_buf[b, :, hh * hd:(hh + 1) * hd] = (yh * _rms(yh, hd)).astype(jnp.bfloat16)

        for gi, w in enumerate(POOL_WINDOWS):
            lo = gi * gd
            win = v_ext[:, lo:lo + gd]
            k = 1
            while k < w:
                win = win + (pltpu.roll(win, k, axis=0) if k % 8 else
                             jnp.concatenate([win[:k], win[:-k]], axis=0))
                k *= 2
            cnt = jnp.minimum(pos, w).astype(jnp.float32)
            pooled = (win[HALO:, :] / cnt - v[:, lo:lo + gd]).astype(jnp.bfloat16)
            yp = jnp.dot(pooled, pool_w[gi], preferred_element_type=jnp.float32)
            yp = yp * _rms(yp, gd) * pool_scale_ref[:, lo:lo + gd]
            mixed_buf[b, :, cw + lo:cw + lo + gd] = yp.astype(jnp.bfloat16)

    for b in range(n_rows):
        mix_out = jnp.dot(mixed_buf[b], w_out_ref[...], preferred_element_type=jnp.float32)
        o_ref[b] = xs[b] + mix_out * _rms(mix_out, d_model) * g_post_ref[...]


def _mixer(x, g_pre, w_in, conv_w, pool_w, pool_scale, w_out, g_post, w_gate, w_up, w_down):
    bn, seq, d_model = x.shape
    ts = MIXER_SEQ_TILE
    cw = conv_w.shape[-1]
    pw = pool_scale.shape[-1]
    d_ff = w_gate.shape[-1]
    assert seq % ts == 0
    n_steps = seq // ts
    gu_rows = d_model // n_steps
    wd_rows = d_ff // n_steps
    assert gu_rows * n_steps == d_model and gu_rows % 16 == 0
    assert wd_rows * n_steps == d_ff and wd_rows % 16 == 0

    def const(shape):
        return pl.BlockSpec(shape, lambda s: (0,) * len(shape), pipeline_mode=pl.Buffered(1))

    def rows_map(s):
        return (s, 0)

    kern = functools.partial(_mixer_kernel, ts=ts, d_model=d_model, cw=cw, pw=pw)
    bf = jnp.bfloat16
    return pl.pallas_call(
        kern,
        grid=(n_steps,),
        in_specs=[
            pl.BlockSpec((bn, ts, d_model), lambda s: (0, s, 0)),
            const((1, d_model)),
            const(w_in.shape),
            const(conv_w.shape),
            const(pool_w.shape),
            const((1, pw)),
            const(w_out.shape),
            const((1, d_model)),
            pl.BlockSpec((gu_rows, d_ff), rows_map),
            pl.BlockSpec((gu_rows, d_ff), rows_map),
            pl.BlockSpec((wd_rows, d_model), rows_map),
        ],
        out_specs=[
            pl.BlockSpec((bn, ts, d_model), lambda s: (0, s, 0)),
            pl.BlockSpec((gu_rows, d_ff), rows_map),
            pl.BlockSpec((gu_rows, d_ff), rows_map),
            pl.BlockSpec((wd_rows, d_model), rows_map),
        ],
        out_shape=[
            jax.ShapeDtypeStruct(x.shape, x.dtype),
            jax.ShapeDtypeStruct(w_gate.shape, bf),
            jax.ShapeDtypeStruct(w_up.shape, bf),
            jax.ShapeDtypeStruct(w_down.shape, bf),
        ],
        scratch_shapes=[
            pltpu.VMEM((bn, HALO, cw), jnp.float32),
            pltpu.VMEM((bn, HALO, pw), jnp.float32),
            pltpu.VMEM((bn, ts, cw + pw), jnp.bfloat16),
        ],
        compiler_params=pltpu.CompilerParams(
            dimension_semantics=("arbitrary",),
            vmem_limit_bytes=V7X_VMEM_LIMIT_BYTES),
        name="mixer",
    )(x, g_pre.reshape(1, -1), w_in, conv_w, pool_w, pool_scale.reshape(1, -1), w_out,
      g_post.reshape(1, -1), w_gate, w_up, w_down)


def _ffn_kernel(x_ref, g_pre_ref, wg_ref, wu_ref, wd_ref, g_post_ref, o_ref, hf_buf,
                *, d_model, tf, short, short_pos):
    j = pl.program_id(1)
    last = pl.num_programs(1) - 1

    def partial_ffn(hf, width):
        g = jnp.dot(hf, wg_ref[:, :width], preferred_element_type=jnp.float32)
        up = jnp.dot(hf, wu_ref[:, :width], preferred_element_type=jnp.float32)
        act = (g * jax.nn.sigmoid(g) * up).astype(jnp.bfloat16)
        return [(n, jnp.dot(act, wd_ref[:width, n:n + FFN_DOWN_COLS],
                            preferred_element_type=jnp.float32))
                for n in range(0, d_model, FFN_DOWN_COLS)]

    tm = x_ref.shape[0]
    groups = [slice(r, r + tm // FFN_EDGE_GROUPS) for r in range(0, tm, tm // FFN_EDGE_GROUPS)]

    @pl.when(j == 0)
    def _():
        for rows in groups:
            x = x_ref[rows, :]
            hf = (x * _rms(x, d_model) * g_pre_ref[...]).astype(jnp.bfloat16)
            hf_buf[rows, :] = hf
            for n, part in partial_ffn(hf, tf):
                o_ref[rows, n:n + FFN_DOWN_COLS] = part

    def accumulate(width):
        for n, part in partial_ffn(hf_buf[...], width):
            o_ref[:, n:n + FFN_DOWN_COLS] += part

    if short_pos is None:
        pl.when(jnp.logical_and(j > 0, j < last))(lambda: accumulate(tf))
    else:
        pl.when(jnp.logical_and(jnp.logical_and(j > 0, j < last), j != short_pos))(
            lambda: accumulate(tf))
        pl.when(j == short_pos)(lambda: accumulate(short))

    @pl.when(j == last)
    def _():
        for rows in groups:
            parts = [part for _, part in partial_ffn(hf_buf[rows, :], tf)]
            ff = o_ref[rows, :] + jnp.concatenate(parts, axis=1)
            o_ref[rows, :] = x_ref[rows, :] + ff * _rms(ff, d_model) * g_post_ref[...]


def _ffn(x, g_pre, w_gate, w_up, w_down, g_post):
    m, d_model = x.shape
    tm, tf = FFN_ROW_TILE, FFN_DFF_BLOCK
    d_ff = w_gate.shape[-1]
    n_blocks = pl.cdiv(d_ff, tf)
    short = d_ff - (n_blocks - 1) * tf
    short_pos = None if short == tf else n_blocks // 2
    assert m % tm == 0 and n_blocks >= 3 and short % 256 == 0

    def block_of(j):
        if short_pos is None:
            return j
        return jnp.where(j < short_pos, j, jnp.where(j == short_pos, n_blocks - 1, j - 1))

    kern = functools.partial(_ffn_kernel, d_model=d_model, tf=tf, short=short, short_pos=short_pos)
    return pl.pallas_call(
        kern,
        grid=(m // tm, n_blocks),
        in_specs=[
            pl.BlockSpec((tm, d_model), lambda i, j: (i, 0)),
            pl.BlockSpec((1, d_model), lambda i, j: (0, 0)),
            pl.BlockSpec((d_model, tf), lambda i, j: (0, block_of(j))),
            pl.BlockSpec((d_model, tf), lambda i, j: (0, block_of(j))),
            pl.BlockSpec((tf, d_model), lambda i, j: (block_of(j), 0)),
            pl.BlockSpec((1, d_model), lambda i, j: (0, 0)),
        ],
        out_specs=pl.BlockSpec((tm, d_model), lambda i, j: (i, 0)),
        out_shape=jax.ShapeDtypeStruct(x.shape, x.dtype),
        scratch_shapes=[pltpu.VMEM((tm, d_model), jnp.bfloat16)],
        compiler_params=pltpu.CompilerParams(
            dimension_semantics=("arbitrary", "arbitrary"),
            vmem_limit_bytes=V7X_VMEM_LIMIT_BYTES),
        name="ffn",
    )(x, g_pre.reshape(1, -1), w_gate, w_up, w_down, g_post.reshape(1, -1))


def kernel(x, ln_mix_pre, w_in, conv_w, pool_w, pool_scale, w_out, ln_mix_post, ln_ffn_pre,
           w_gate, w_up, w_down, ln_ffn_post):
    bn, seq, d_model = x.shape
    depth = w_in.shape[0]
    bf = jnp.bfloat16
    for l in range(depth):
        x, wg, wu, wd = _mixer(x, ln_mix_pre[l], w_in[l].astype(bf), conv_w[l], pool_w[l],
                               pool_scale[l], w_out[l].astype(bf), ln_mix_post[l],
                               w_gate[l], w_up[l], w_down[l])
        y = _ffn(x.reshape(bn * seq, d_model), ln_ffn_pre[l], wg, wu, wd, ln_ffn_post[l])
        x = y.reshape(bn, seq, d_model)
    return x
```

```python
import functools

import jax
import jax.numpy as jnp
from jax import lax
from jax.experimental import pallas as pl
from jax.experimental.pallas import tpu as pltpu

EPS = 1e-6
CONV_HEADS = 8
CONV_K = 3
POOL_WINDOWS = (2, 4, 8, 16)
HALO = 16

V7X_VMEM_LIMIT_BYTES = 62 * 1024 * 1024
FFN_DOWN_COLS = 512
FFN_EDGE_GROUPS = 2

MIXER_SEQ_TILE = 256
FFN_ROW_TILE = 1024
FFN_DFF_BLOCK = 768


def _rms(xf, axis_size):
    return lax.rsqrt(jnp.sum(xf * xf, axis=-1, keepdims=True) * (1.0 / axis_size) + EPS)


def _mixer_kernel(x_ref, g_pre_ref, w_in_ref, conv_w_ref, pool_w_ref, pool_scale_ref,
                  w_out_ref, g_post_ref, wg_ref, wu_ref, wd_ref,
                  o_ref, wg_o, wu_o, wd_o,
                  cu_hist, v_hist, mixed_buf, *, ts, d_model, cw, pw):
    s = pl.program_id(0)
    n_groups = len(POOL_WINDOWS)
    gd = pw // n_groups
    hd = cw // CONV_HEADS

    @pl.when(s == 0)
    def _():
        cu_hist[...] = jnp.zeros_like(cu_hist)
        v_hist[...] = jnp.zeros_like(v_hist)

    pos = s * ts + lax.broadcasted_iota(jnp.int32, (ts, 1), 0) + 1

    n_rows = x_ref.shape[0]
    xs = [x_ref[b] for b in range(n_rows)]
    h = jnp.concatenate(
        [(x * _rms(x, d_model) * g_pre_ref[...]).astype(jnp.bfloat16) for x in xs], axis=0)

    def proj(lo, width):
        return jnp.dot(h, w_in_ref[:, lo:lo + width], preferred_element_type=jnp.float32)

    v_all = proj(3 * cw, pw)
    wg_o[...] = wg_ref[...].astype(jnp.bfloat16)
    wu_o[...] = wu_ref[...].astype(jnp.bfloat16)
    wd_o[...] = wd_ref[...].astype(jnp.bfloat16)
    cu_all = proj(cw, cw) * proj(2 * cw, cw)
    b_all = proj(0, cw)

    pool_w = [pool_w_ref[gi].astype(jnp.bfloat16) for gi in range(n_groups)]

    for b in range(n_rows):
        rows = slice(b * ts, (b + 1) * ts)
        v = v_all[rows]
        v_ext = jnp.concatenate([v_hist[b], v], axis=0)
        v_hist[b] = v[ts - HALO:, :]

        cu = cu_all[rows]
        cu_ext = jnp.concatenate([cu_hist[b], cu], axis=0)
        cu_hist[b] = cu[ts - HALO:, :]
        conv = conv_w_ref[CONV_K - 1:CONV_K, :] * cu
        for k in range(CONV_K - 1):
            delay = CONV_K - 1 - k
            conv = conv + conv_w_ref[k:k + 1, :] * pltpu.roll(cu_ext, delay, axis=0)[HALO:, :]
        y_conv = b_all[rows] * conv
        for hh in range(CONV_HEADS):
            yh = y_conv[:, hh * hd:(hh + 1) * hd]
            mixed_buf[b, :, hh * hd:(hh + 1) * hd] = (yh * _rms(yh, hd)).astype(jnp.bfloat16)

        for gi, w in enumerate(POOL_WINDOWS):
            lo = gi * gd
            win = v_ext[:, lo:lo + gd]
            k = 1
            while k < w:
                win = win + (pltpu.roll(win, k, axis=0) if k % 8 else
                             jnp.concatenate([win[:k], win[:-k]], axis=0))
                k *= 2
            cnt = jnp.minimum(pos, w).astype(jnp.float32)
            pooled = (win[HALO:, :] / cnt - v[:, lo:lo + gd]).astype(jnp.bfloat16)
            yp = jnp.dot(pooled, pool_w[gi], preferred_element_type=jnp.float32)
            yp = yp * _rms(yp, gd) * pool_scale_ref[:, lo:lo + gd]
            mixed_buf[b, :, cw + lo:cw + lo + gd] = yp.astype(jnp.bfloat16)

    for b in range(n_rows):
        mix_out = jnp.dot(mixed_buf[b], w_out_ref[...], preferred_element_type=jnp.float32)
        o_ref[b] = xs[b] + mix_out * _rms(mix_out, d_model) * g_post_ref[...]


def _mixer(x, g_pre, w_in, conv_w_all, layer, pool_w, pool_scale, w_out, g_post, w_gate, w_up, w_down):
    bn, seq, d_model = x.shape
    ts = MIXER_SEQ_TILE
    cw = conv_w_all.shape[-1]
    pw = pool_scale.shape[-1]
    d_ff = w_gate.shape[-1]
    assert seq % ts == 0
    n_steps = seq // ts
    gu_rows = d_model // n_steps
    wd_rows = d_ff // n_steps
    assert gu_rows * n_steps == d_model and gu_rows % 16 == 0
    assert wd_rows * n_steps == d_ff and wd_rows % 16 == 0

    def const(shape):
        return pl.BlockSpec(shape, lambda s: (0,) * len(shape), pipeline_mode=pl.Buffered(1))

    def rows_map(s):
        return (s, 0)

    kern = functools.partial(_mixer_kernel, ts=ts, d_model=d_model, cw=cw, pw=pw)
    bf = jnp.bfloat16
    return pl.pallas_call(
        kern,
        grid=(n_steps,),
        in_specs=[
            pl.BlockSpec((bn, ts, d_model), lambda s: (0, s, 0)),
            const((1, d_model)),
            const(w_in.shape),
            pl.BlockSpec((None,) + conv_w_all.shape[1:], lambda s: (layer, 0, 0),
                         pipeline_mode=pl.Buffered(1)),
            const(pool_w.shape),
            const((1, pw)),
            const(w_out.shape),
            const((1, d_model)),
            pl.BlockSpec((gu_rows, d_ff), rows_map),
            pl.BlockSpec((gu_rows, d_ff), rows_map),
            pl.BlockSpec((wd_rows, d_model), rows_map),
        ],
        out_specs=[
            pl.BlockSpec((bn, ts, d_model), lambda s: (0, s, 0)),
            pl.BlockSpec((gu_rows, d_ff), rows_map),
            pl.BlockSpec((gu_rows, d_ff), rows_map),
            pl.BlockSpec((wd_rows, d_model), rows_map),
        ],
        out_shape=[
            jax.ShapeDtypeStruct(x.shape, x.dtype),
            jax.ShapeDtypeStruct(w_gate.shape, bf),
            jax.ShapeDtypeStruct(w_up.shape, bf),
            jax.ShapeDtypeStruct(w_down.shape, bf),
        ],
        scratch_shapes=[
            pltpu.VMEM((bn, HALO, cw), jnp.float32),
            pltpu.VMEM((bn, HALO, pw), jnp.float32),
            pltpu.VMEM((bn, ts, cw + pw), jnp.bfloat16),
        ],
        compiler_params=pltpu.CompilerParams(
            dimension_semantics=("arbitrary",),
            vmem_limit_bytes=V7X_VMEM_LIMIT_BYTES),
        name="mixer",
    )(x, g_pre.reshape(1, -1), w_in, conv_w_all, pool_w, pool_scale.reshape(1, -1), w_out,
      g_post.reshape(1, -1), w_gate, w_up, w_down)


def _ffn_kernel(x_ref, g_pre_ref, wg_ref, wu_ref, wd_ref, g_post_ref, o_ref, hf_buf,
                *, d_model, tf, short, short_pos):
    j = pl.program_id(1)
    last = pl.num_programs(1) - 1

    def partial_ffn(hf, width):
        g = jnp.dot(hf, wg_ref[:, :width], preferred_element_type=jnp.float32)
        up = jnp.dot(hf, wu_ref[:, :width], preferred_element_type=jnp.float32)
        act = (g * jax.nn.sigmoid(g) * up).astype(jnp.bfloat16)
        return [(n, jnp.dot(act, wd_ref[:width, n:n + FFN_DOWN_COLS],
                            preferred_element_type=jnp.float32))
                for n in range(0, d_model, FFN_DOWN_COLS)]

    tm = x_ref.shape[0]
    groups = [slice(r, r + tm // FFN_EDGE_GROUPS) for r in range(0, tm, tm // FFN_EDGE_GROUPS)]

    @pl.when(j == 0)
    def _():
        for rows in groups:
            x = x_ref[rows, :]
            hf = (x * _rms(x, d_model) * g_pre_ref[...]).astype(jnp.bfloat16)
            hf_buf[rows, :] = hf
            for n, part in partial_ffn(hf, tf):
                o_ref[rows, n:n + FFN_DOWN_COLS] = part

    def accumulate(width):
        for n, part in partial_ffn(hf_buf[...], width):
            o_ref[:, n:n + FFN_DOWN_COLS] += part

    if short_pos is None:
        pl.when(jnp.logical_and(j > 0, j < last))(lambda: accumulate(tf))
    else:
        pl.when(jnp.logical_and(jnp.logical_and(j > 0, j < last), j != short_pos))(
            lambda: accumulate(tf))
        pl.when(j == short_pos)(lambda: accumulate(short))

    @pl.when(j == last)
    def _():
        for rows in groups:
            parts = [part for _, part in partial_ffn(hf_buf[rows, :], tf)]
            ff = o_ref[rows, :] + jnp.concatenate(parts, axis=1)
            o_ref[rows, :] = x_ref[rows, :] + ff * _rms(ff, d_model) * g_post_ref[...]


def _ffn(x, g_pre, w_gate, w_up, w_down, g_post):
    m, d_model = x.shape
    tm, tf = FFN_ROW_TILE, FFN_DFF_BLOCK
    d_ff = w_gate.shape[-1]
    n_blocks = pl.cdiv(d_ff, tf)
    short = d_ff - (n_blocks - 1) * tf
    short_pos = None if short == tf else n_blocks // 2
    assert m % tm == 0 and n_blocks >= 3 and short % 256 == 0

    def block_of(j):
        if short_pos is None:
            return j
        return jnp.where(j < short_pos, j, jnp.where(j == short_pos, n_blocks - 1, j - 1))

    kern = functools.partial(_ffn_kernel, d_model=d_model, tf=tf, short=short, short_pos=short_pos)
    return pl.pallas_call(
        kern,
        grid=(m // tm, n_blocks),
        in_specs=[
            pl.BlockSpec((tm, d_model), lambda i, j: (i, 0)),
            pl.BlockSpec((1, d_model), lambda i, j: (0, 0)),
            pl.BlockSpec((d_model, tf), lambda i, j: (0, block_of(j))),
            pl.BlockSpec((d_model, tf), lambda i, j: (0, block_of(j))),
            pl.BlockSpec((tf, d_model), lambda i, j: (block_of(j), 0)),
            pl.BlockSpec((1, d_model), lambda i, j: (0, 0)),
        ],
        out_specs=pl.BlockSpec((tm, d_model), lambda i, j: (i, 0)),
        out_shape=jax.ShapeDtypeStruct(x.shape, x.dtype),
        scratch_shapes=[pltpu.VMEM((tm, d_model), jnp.bfloat16)],
        compiler_params=pltpu.CompilerParams(
            dimension_semantics=("arbitrary", "arbitrary"),
            vmem_limit_bytes=V7X_VMEM_LIMIT_BYTES),
        name="ffn",
    )(x, g_pre.reshape(1, -1), w_gate, w_up, w_down, g_post.reshape(1, -1))


def kernel(x, ln_mix_pre, w_in, conv_w, pool_w, pool_scale, w_out, ln_mix_post, ln_ffn_pre,
           w_gate, w_up, w_down, ln_ffn_post):
    bn, seq, d_model = x.shape
    depth = w_in.shape[0]
    bf = jnp.bfloat16
    for l in range(depth):
        x, wg, wu, wd = _mixer(x, ln_mix_pre[l], w_in[l].astype(bf), conv_w, l, pool_w[l],
                               pool_scale[l], w_out[l].astype(bf), ln_mix_post[l],
                               w_gate[l], w_up[l], w_down[l])
        y = _ffn(x.reshape(bn * seq, d_model), ln_ffn_pre[l], wg, wu, wd, ln_ffn_post[l])
        x = y.reshape(bn, seq, d_model)
    return x
```

```python
import functools

import jax
import jax.numpy as jnp
from jax import lax
from jax.experimental import pallas as pl
from jax.experimental.pallas import tpu as pltpu

EPS = 1e-6
CONV_HEADS = 8
CONV_K = 3
POOL_WINDOWS = (2, 4, 8, 16)
HALO = 16

V7X_VMEM_LIMIT_BYTES = 62 * 1024 * 1024

MIXER_SEQ_TILE = 256
FFN_ROW_TILE = 256


def _rms(xf, axis_size):
    return lax.rsqrt(jnp.sum(xf * xf, axis=-1, keepdims=True) * (1.0 / axis_size) + EPS)


def _mixer_kernel(x_ref, g_pre_ref, w_in_ref, conv_w_ref, pool_w_ref, pool_scale_ref,
                  w_out_ref, g_post_ref, wg_ref, wu_ref, wd_ref,
                  o_ref, wg_o, wu_o, wd_o,
                  cu_hist, v_hist, mixed_buf, *, ts, d_model, cw, pw):
    s = pl.program_id(0)
    n_groups = len(POOL_WINDOWS)
    gd = pw // n_groups
    hd = cw // CONV_HEADS

    @pl.when(s == 0)
    def _():
        cu_hist[...] = jnp.zeros_like(cu_hist)
        v_hist[...] = jnp.zeros_like(v_hist)

    pos = s * ts + lax.broadcasted_iota(jnp.int32, (ts, 1), 0) + 1

    n_rows = x_ref.shape[0]
    xs = [x_ref[b] for b in range(n_rows)]
    h = jnp.concatenate(
        [(x * _rms(x, d_model) * g_pre_ref[...]).astype(jnp.bfloat16) for x in xs], axis=0)

    def proj(lo, width):
        return jnp.dot(h, w_in_ref[:, lo:lo + width], preferred_element_type=jnp.float32)

    v_all = proj(3 * cw, pw)
    wg_o[...] = wg_ref[...].astype(jnp.bfloat16)
    wu_o[...] = wu_ref[...].astype(jnp.bfloat16)
    wd_o[...] = wd_ref[...].astype(jnp.bfloat16)
    cu_all = proj(cw, cw) * proj(2 * cw, cw)
    b_all = proj(0, cw)

    pool_w = [pool_w_ref[gi].astype(jnp.bfloat16) for gi in range(n_groups)]

    for b in range(n_rows):
        rows = slice(b * ts, (b + 1) * ts)
        v = v_all[rows]
        v_ext = jnp.concatenate([v_hist[b], v], axis=0)
        v_hist[b] = v[ts - HALO:, :]

        cu = cu_all[rows]
        cu_ext = jnp.concatenate([cu_hist[b], cu], axis=0)
        cu_hist[b] = cu[ts - HALO:, :]
        conv = conv_w_ref[CONV_K - 1:CONV_K, :] * cu
        for k in range(CONV_K - 1):
            delay = CONV_K - 1 - k
            conv = conv + conv_w_ref[k:k + 1, :] * pltpu.roll(cu_ext, delay, axis=0)[HALO:, :]
        y_conv = b_all[rows] * conv
        for hh in range(CONV_HEADS):
            yh = y_conv[:, hh * hd:(hh + 1) * hd]
            mixed_buf[b, :, hh * hd:(hh + 1) * hd] = (yh * _rms(yh, hd)).astype(jnp.bfloat16)

        for gi, w in enumerate(POOL_WINDOWS):
            lo = gi * gd
            win = v_ext[:, lo:lo + gd]
            k = 1
            while k < w:
                win = win + (pltpu.roll(win, k, axis=0) if k % 8 else
                             jnp.concatenate([win[:k], win[:-k]], axis=0))
                k *= 2
            cnt = jnp.minimum(pos, w).astype(jnp.float32)
            pooled = (win[HALO:, :] / cnt - v[:, lo:lo + gd]).astype(jnp.bfloat16)
            yp = jnp.dot(pooled, pool_w[gi], preferred_element_type=jnp.float32)
            yp = yp * _rms(yp, gd) * pool_scale_ref[:, lo:lo + gd]
            mixed_buf[b, :, cw + lo:cw + lo + gd] = yp.astype(jnp.bfloat16)

    for b in range(n_rows):
        mix_out = jnp.dot(mixed_buf[b], w_out_ref[...], preferred_element_type=jnp.float32)
        o_ref[b] = xs[b] + mix_out * _rms(mix_out, d_model) * g_post_ref[...]


def _mixer(x, g_pre, w_in, conv_w_all, layer, pool_w, pool_scale, w_out, g_post, w_gate, w_up, w_down):
    bn, seq, d_model = x.shape
    ts = MIXER_SEQ_TILE
    cw = conv_w_all.shape[-1]
    pw = pool_scale.shape[-1]
    d_ff = w_gate.shape[-1]
    assert seq % ts == 0
    n_steps = seq // ts
    gu_rows = d_model // n_steps
    wd_rows = d_ff // n_steps
    assert gu_rows * n_steps == d_model and gu_rows % 16 == 0
    assert wd_rows * n_steps == d_ff and wd_rows % 16 == 0

    def const(shape):
        return pl.BlockSpec(shape, lambda s: (0,) * len(shape), pipeline_mode=pl.Buffered(1))

    def rows_map(s):
        return (s, 0)

    kern = functools.partial(_mixer_kernel, ts=ts, d_model=d_model, cw=cw, pw=pw)
    bf = jnp.bfloat16
    return pl.pallas_call(
        kern,
        grid=(n_steps,),
        in_specs=[
            pl.BlockSpec((bn, ts, d_model), lambda s: (0, s, 0)),
            const((1, d_model)),
            const(w_in.shape),
            pl.BlockSpec((None,) + conv_w_all.shape[1:], lambda s: (layer, 0, 0),
                         pipeline_mode=pl.Buffered(1)),
            const(pool_w.shape),
            const((1, pw)),
            const(w_out.shape),
            const((1, d_model)),
            pl.BlockSpec((gu_rows, d_ff), rows_map),
            pl.BlockSpec((gu_rows, d_ff), rows_map),
            pl.BlockSpec((wd_rows, d_model), rows_map),
        ],
        out_specs=[
            pl.BlockSpec((bn, ts, d_model), lambda s: (0, s, 0)),
            pl.BlockSpec((gu_rows, d_ff), rows_map),
            pl.BlockSpec((gu_rows, d_ff), rows_map),
            pl.BlockSpec((wd_rows, d_model), rows_map),
        ],
        out_shape=[
            jax.ShapeDtypeStruct(x.shape, x.dtype),
            jax.ShapeDtypeStruct(w_gate.shape, bf),
            jax.ShapeDtypeStruct(w_up.shape, bf),
            jax.ShapeDtypeStruct(w_down.shape, bf),
        ],
        scratch_shapes=[
            pltpu.VMEM((bn, HALO, cw), jnp.float32),
            pltpu.VMEM((bn, HALO, pw), jnp.float32),
            pltpu.VMEM((bn, ts, cw + pw), jnp.bfloat16),
        ],
        compiler_params=pltpu.CompilerParams(
            dimension_semantics=("arbitrary",),
            vmem_limit_bytes=V7X_VMEM_LIMIT_BYTES),
        name="mixer",
    )(x, g_pre.reshape(1, -1), w_in, conv_w_all, pool_w, pool_scale.reshape(1, -1), w_out,
      g_post.reshape(1, -1), w_gate, w_up, w_down)


def _ffn_half_kernel(*refs, d_model, final):
    if final:
        x_ref, p_ref, g_pre_ref, wg_ref, wu_ref, wd_ref, g_post_ref, o_ref = refs
    else:
        x_ref, g_pre_ref, wg_ref, wu_ref, wd_ref, o_ref = refs
    x = x_ref[...]
    hf = (x * _rms(x, d_model) * g_pre_ref[...]).astype(jnp.bfloat16)
    g = jnp.dot(hf, wg_ref[...], preferred_element_type=jnp.float32)
    up = jnp.dot(hf, wu_ref[...], preferred_element_type=jnp.float32)
    act = (g * jax.nn.sigmoid(g) * up).astype(jnp.bfloat16)
    part = jnp.dot(act, wd_ref[...], preferred_element_type=jnp.float32)
    if final:
        ff = p_ref[...] + part
        o_ref[...] = x + ff * _rms(ff, d_model) * g_post_ref[...]
    else:
        o_ref[...] = part


def _ffn_half(x, partial, g_pre, w_gate, w_up, w_down, g_post, half):
    m, d_model = x.shape
    tm = FFN_ROW_TILE
    d_ff = w_gate.shape[-1]
    assert m % tm == 0 and d_ff % 256 == 0
    hw = d_ff // 2
    final = partial is not None
    rows = pl.BlockSpec((tm, d_model), lambda i: (i, 0))
    vec = pl.BlockSpec((1, d_model), lambda i: (0, 0))
    once = pl.Buffered(1)
    w_specs = [pl.BlockSpec((d_model, hw), lambda i: (0, half), pipeline_mode=once),
               pl.BlockSpec((d_model, hw), lambda i: (0, half), pipeline_mode=once),
               pl.BlockSpec((hw, d_model), lambda i: (half, 0), pipeline_mode=once)]
    if final:
        in_specs = [rows, rows, vec] + w_specs + [vec]
        args = (x, partial, g_pre.reshape(1, -1), w_gate, w_up, w_down, g_post.reshape(1, -1))
    else:
        in_specs = [rows, vec] + w_specs
        args = (x, g_pre.reshape(1, -1), w_gate, w_up, w_down)
    return pl.pallas_call(
        functools.partial(_ffn_half_kernel, d_model=d_model, final=final),
        grid=(m // tm,),
        in_specs=in_specs,
        out_specs=rows,
        out_shape=jax.ShapeDtypeStruct(x.shape, x.dtype),
        compiler_params=pltpu.CompilerParams(
            dimension_semantics=("arbitrary",),
            vmem_limit_bytes=V7X_VMEM_LIMIT_BYTES),
        name="ffn_final" if final else "ffn_partial",
    )(*args)


def _ffn(x, g_pre, w_gate, w_up, w_down, g_post):
    partial = _ffn_half(x, None, g_pre, w_gate, w_up, w_down, g_post, 0)
    return _ffn_half(x, partial, g_pre, w_gate, w_up, w_down, g_post, 1)


def kernel(x, ln_mix_pre, w_in, conv_w, pool_w, pool_scale, w_out, ln_mix_post, ln_ffn_pre,
           w_gate, w_up, w_down, ln_ffn_post):
    bn, seq, d_model = x.shape
    depth = w_in.shape[0]
    bf = jnp.bfloat16
    for l in range(depth):
        x, wg, wu, wd = _mixer(x, ln_mix_pre[l], w_in[l].astype(bf), conv_w, l, pool_w[l],
                               pool_scale[l], w_out[l].astype(bf), ln_mix_post[l],
                               w_gate[l], w_up[l], w_down[l])
        y = _ffn(x.reshape(bn * seq, d_model), ln_ffn_pre[l], wg, wu, wd, ln_ffn_post[l])
        x = y.reshape(bn, seq, d_model)
    return x
```

```python
import functools

import jax
import jax.numpy as jnp
from jax import lax
from jax.experimental import pallas as pl
from jax.experimental.pallas import tpu as pltpu

EPS = 1e-6
CONV_HEADS = 8
CONV_K = 3
POOL_WINDOWS = (2, 4, 8, 16)
HALO = 16

V7X_VMEM_LIMIT_BYTES = 62 * 1024 * 1024
FFN_DOWN_COLS = 512
FFN_EDGE_GROUPS = 2

MIXER_SEQ_TILE = 256
OUTPROJ_ROW_TILE = 1024
OUTPROJ_GROUP_ROWS = 256
FFN_ROW_TILE = 1024
FFN_DFF_BLOCK = 768


def _rms(xf, axis_size):
    return lax.rsqrt(jnp.sum(xf * xf, axis=-1, keepdims=True) * (1.0 / axis_size) + EPS)


def _mixer_kernel(x_ref, g_pre_ref, w_in_ref, conv_w_ref, pool_w_ref, pool_scale_ref,
                  wg_ref, wu_ref, wd_ref,
                  mixed_o, wg_o, wu_o, wd_o,
                  cu_hist, v_hist, *, ts, d_model, cw, pw):
    s = pl.program_id(0)
    n_groups = len(POOL_WINDOWS)
    gd = pw // n_groups
    hd = cw // CONV_HEADS

    @pl.when(s == 0)
    def _():
        cu_hist[...] = jnp.zeros_like(cu_hist)
        v_hist[...] = jnp.zeros_like(v_hist)

    pos = s * ts + lax.broadcasted_iota(jnp.int32, (ts, 1), 0) + 1

    n_rows = x_ref.shape[0]
    xs = [x_ref[b] for b in range(n_rows)]
    h = jnp.concatenate(
        [(x * _rms(x, d_model) * g_pre_ref[...]).astype(jnp.bfloat16) for x in xs], axis=0)

    def proj(lo, width):
        return jnp.dot(h, w_in_ref[:, lo:lo + width], preferred_element_type=jnp.float32)

    v_all = proj(3 * cw, pw)
    wg_o[...] = wg_ref[...].astype(jnp.bfloat16)
    wu_o[...] = wu_ref[...].astype(jnp.bfloat16)
    wd_o[...] = wd_ref[...].astype(jnp.bfloat16)
    cu_all = proj(cw, cw) * proj(2 * cw, cw)
    b_all = proj(0, cw)

    pool_w = [pool_w_ref[gi].astype(jnp.bfloat16) for gi in range(n_groups)]

    for b in range(n_rows):
        rows = slice(b * ts, (b + 1) * ts)
        v = v_all[rows]
        v_ext = jnp.concatenate([v_hist[b], v], axis=0)
        v_hist[b] = v[ts - HALO:, :]

        cu = cu_all[rows]
        cu_ext = jnp.concatenate([cu_hist[b], cu], axis=0)
        cu_hist[b] = cu[ts - HALO:, :]
        conv = conv_w_ref[CONV_K - 1:CONV_K, :] * cu
        for k in range(CONV_K - 1):
            delay = CONV_K - 1 - k
            conv = conv + conv_w_ref[k:k + 1, :] * pltpu.roll(cu_ext, delay, axis=0)[HALO:, :]
        y_conv = b_all[rows] * conv
        for hh in range(CONV_HEADS):
            yh = y_conv[:, hh * hd:(hh + 1) * hd]
            mixed_o[b, :, hh * hd:(hh + 1) * hd] = (yh * _rms(yh, hd)).astype(jnp.bfloat16)

        for gi, w in enumerate(POOL_WINDOWS):
            lo = gi * gd
            win = v_ext[:, lo:lo + gd]
            k = 1
            while k < w:
                win = win + (pltpu.roll(win, k, axis=0) if k % 8 else
                             jnp.concatenate([win[:k], win[:-k]], axis=0))
                k *= 2
            cnt = jnp.minimum(pos, w).astype(jnp.float32)
            pooled = (win[HALO:, :] / cnt - v[:, lo:lo + gd]).astype(jnp.bfloat16)
            yp = jnp.dot(pooled, pool_w[gi], preferred_element_type=jnp.float32)
            yp = yp * _rms(yp, gd) * pool_scale_ref[:, lo:lo + gd]
            mixed_o[b, :, cw + lo:cw + lo + gd] = yp.astype(jnp.bfloat16)


def _mixer(x, g_pre, w_in, conv_w_all, layer, pool_w, pool_scale, w_gate, w_up, w_down):
    bn, seq, d_model = x.shape
    ts = MIXER_SEQ_TILE
    cw = conv_w_all.shape[-1]
    pw = pool_scale.shape[-1]
    d_ff = w_gate.shape[-1]
    assert seq % ts == 0
    n_steps = seq // ts
    gu_rows = d_model // n_steps
    wd_rows = d_ff // n_steps
    assert gu_rows * n_steps == d_model and gu_rows % 16 == 0
    assert wd_rows * n_steps == d_ff and wd_rows % 16 == 0

    def const(shape):
        return pl.BlockSpec(shape, lambda s: (0,) * len(shape), pipeline_mode=pl.Buffered(1))

    def rows_map(s):
        return (s, 0)

    kern = functools.partial(_mixer_kernel, ts=ts, d_model=d_model, cw=cw, pw=pw)
    bf = jnp.bfloat16
    return pl.pallas_call(
        kern,
        grid=(n_steps,),
        in_specs=[
            pl.BlockSpec((bn, ts, d_model), lambda s: (0, s, 0)),
            const((1, d_model)),
            const(w_in.shape),
            pl.BlockSpec((None,) + conv_w_all.shape[1:], lambda s: (layer, 0, 0),
                         pipeline_mode=pl.Buffered(1)),
            const(pool_w.shape),
            const((1, pw)),
            pl.BlockSpec((gu_rows, d_ff), rows_map),
            pl.BlockSpec((gu_rows, d_ff), rows_map),
            pl.BlockSpec((wd_rows, d_model), rows_map),
        ],
        out_specs=[
            pl.BlockSpec((bn, ts, cw + pw), lambda s: (0, s, 0)),
            pl.BlockSpec((gu_rows, d_ff), rows_map),
            pl.BlockSpec((gu_rows, d_ff), rows_map),
            pl.BlockSpec((wd_rows, d_model), rows_map),
        ],
        out_shape=[
            jax.ShapeDtypeStruct((bn, seq, cw + pw), bf),
            jax.ShapeDtypeStruct(w_gate.shape, bf),
            jax.ShapeDtypeStruct(w_up.shape, bf),
            jax.ShapeDtypeStruct(w_down.shape, bf),
        ],
        scratch_shapes=[
            pltpu.VMEM((bn, HALO, cw), jnp.float32),
            pltpu.VMEM((bn, HALO, pw), jnp.float32),
        ],
        compiler_params=pltpu.CompilerParams(
            dimension_semantics=("arbitrary",),
            vmem_limit_bytes=V7X_VMEM_LIMIT_BYTES),
        name="mixer",
    )(x, g_pre.reshape(1, -1), w_in, conv_w_all, pool_w, pool_scale.reshape(1, -1),
      w_gate, w_up, w_down)


def _outproj_kernel(x_ref, mixed_ref, w_out_ref, g_post_ref, o_ref, *, d_model):
    tm = x_ref.shape[0]
    for r in range(0, tm, OUTPROJ_GROUP_ROWS):
        rows = slice(r, r + OUTPROJ_GROUP_ROWS)
        mix_out = jnp.dot(mixed_ref[rows, :], w_out_ref[...], preferred_element_type=jnp.float32)
        o_ref[rows, :] = x_ref[rows, :] + mix_out * _rms(mix_out, d_model) * g_post_ref[...]


def _outproj(x, mixed, w_out, g_post):
    m, d_model = x.shape
    tm = OUTPROJ_ROW_TILE
    assert m % tm == 0 and tm % OUTPROJ_GROUP_ROWS == 0
    return pl.pallas_call(
        functools.partial(_outproj_kernel, d_model=d_model),
        grid=(m // tm,),
        in_specs=[
            pl.BlockSpec((tm, d_model), lambda i: (i, 0)),
            pl.BlockSpec((tm, mixed.shape[-1]), lambda i: (i, 0)),
            pl.BlockSpec(w_out.shape, lambda i: (0, 0), pipeline_mode=pl.Buffered(1)),
            pl.BlockSpec((1, d_model), lambda i: (0, 0)),
        ],
        out_specs=pl.BlockSpec((tm, d_model), lambda i: (i, 0)),
        out_shape=jax.ShapeDtypeStruct(x.shape, x.dtype),
        compiler_params=pltpu.CompilerParams(
            dimension_semantics=("arbitrary",),
            vmem_limit_bytes=V7X_VMEM_LIMIT_BYTES),
        name="outproj",
    )(x, mixed, w_out, g_post.reshape(1, -1))


def _ffn_kernel(x_ref, g_pre_ref, wg_ref, wu_ref, wd_ref, g_post_ref, o_ref, hf_buf,
                *, d_model, tf, short, short_pos):
    j = pl.program_id(1)
    last = pl.num_programs(1) - 1

    def partial_ffn(hf, width):
        g = jnp.dot(hf, wg_ref[:, :width], preferred_element_type=jnp.float32)
        up = jnp.dot(hf, wu_ref[:, :width], preferred_element_type=jnp.float32)
        act = (g * jax.nn.sigmoid(g) * up).astype(jnp.bfloat16)
        return [(n, jnp.dot(act, wd_ref[:width, n:n + FFN_DOWN_COLS],
                            preferred_element_type=jnp.float32))
                for n in range(0, d_model, FFN_DOWN_COLS)]

    tm = x_ref.shape[0]
    groups = [slice(r, r + tm // FFN_EDGE_GROUPS) for r in range(0, tm, tm // FFN_EDGE_GROUPS)]

    @pl.when(j == 0)
    def _():
        for rows in groups:
            x = x_ref[rows, :]
            hf = (x * _rms(x, d_model) * g_pre_ref[...]).astype(jnp.bfloat16)
            hf_buf[rows, :] = hf
            for n, part in partial_ffn(hf, tf):
                o_ref[rows, n:n + FFN_DOWN_COLS] = part

    def accumulate(width):
        for n, part in partial_ffn(hf_buf[...], width):
            o_ref[:, n:n + FFN_DOWN_COLS] += part

    if short_pos is None:
        pl.when(jnp.logical_and(j > 0, j < last))(lambda: accumulate(tf))
    else:
        pl.when(jnp.logical_and(jnp.logical_and(j > 0, j < last), j != short_pos))(
            lambda: accumulate(tf))
        pl.when(j == short_pos)(lambda: accumulate(short))

    @pl.when(j == last)
    def _():
        for rows in groups:
            parts = [part for _, part in partial_ffn(hf_buf[rows, :], tf)]
            ff = o_ref[rows, :] + jnp.concatenate(parts, axis=1)
            o_ref[rows, :] = x_ref[rows, :] + ff * _rms(ff, d_model) * g_post_ref[...]


def _ffn(x, g_pre, w_gate, w_up, w_down, g_post):
    m, d_model = x.shape
    tm, tf = FFN_ROW_TILE, FFN_DFF_BLOCK
    d_ff = w_gate.shape[-1]
    n_blocks = pl.cdiv(d_ff, tf)
    short = d_ff - (n_blocks - 1) * tf
    short_pos = None if short == tf else n_blocks // 2
    assert m % tm == 0 and n_blocks >= 3 and short % 256 == 0

    def block_of(j):
        if short_pos is None:
            return j
        return jnp.where(j < short_pos, j, jnp.where(j == short_pos, n_blocks - 1, j - 1))

    kern = functools.partial(_ffn_kernel, d_model=d_model, tf=tf, short=short, short_pos=short_pos)
    return pl.pallas_call(
        kern,
        grid=(m // tm, n_blocks),
        in_specs=[
            pl.BlockSpec((tm, d_model), lambda i, j: (i, 0)),
            pl.BlockSpec((1, d_model), lambda i, j: (0, 0)),
            pl.BlockSpec((d_model, tf), lambda i, j: (0, block_of(j))),
            pl.BlockSpec((d_model, tf), lambda i, j: (0, block_of(j))),
            pl.BlockSpec((tf, d_model), lambda i, j: (block_of(j), 0)),
            pl.BlockSpec((1, d_model), lambda i, j: (0, 0)),
        ],
        out_specs=pl.BlockSpec((tm, d_model), lambda i, j: (i, 0)),
        out_shape=jax.ShapeDtypeStruct(x.shape, x.dtype),
        scratch_shapes=[pltpu.VMEM((tm, d_model), jnp.bfloat16)],
        compiler_params=pltpu.CompilerParams(
            dimension_semantics=("arbitrary", "arbitrary"),
            vmem_limit_bytes=V7X_VMEM_LIMIT_BYTES),
        name="ffn",
    )(x, g_pre.reshape(1, -1), w_gate, w_up, w_down, g_post.reshape(1, -1))


def kernel(x, ln_mix_pre, w_in, conv_w, pool_w, pool_scale, w_out, ln_mix_post, ln_ffn_pre,
           w_gate, w_up, w_down, ln_ffn_post):
    bn, seq, d_model = x.shape
    depth = w_in.shape[0]
    bf = jnp.bfloat16
    for l in range(depth):
        mixed, wg, wu, wd = _mixer(x, ln_mix_pre[l], w_in[l].astype(bf), conv_w, l, pool_w[l],
                                   pool_scale[l], w_gate[l], w_up[l], w_down[l])
        x1 = _outproj(x.reshape(bn * seq, d_model), mixed.reshape(bn * seq, -1),
                      w_out[l].astype(bf), ln_mix_post[l])
        y = _ffn(x1, ln_ffn_pre[l], wg, wu, wd, ln_ffn_post[l])
        x = y.reshape(bn, seq, d_model)
    return x
```
